```python
import math
import jax, jax.numpy as jnp
from jax import lax
import numpy as np

D_MODEL = 1024
BATCH = 4
SEQ = 8192
DEPTH = 2

MIX_WIDTH = D_MODEL
HEAD_DIM = 64
N_HEADS_DIFF = MIX_WIDTH // (4 * HEAD_DIM)
N_HEADS_FOX = MIX_WIDTH // (2 * HEAD_DIM)
DIFF_WIDTH = N_HEADS_DIFF * 2 * HEAD_DIM
FOX_WIDTH = N_HEADS_FOX * HEAD_DIM
IN_WIDTH = 3 * DIFF_WIDTH + 3 * FOX_WIDTH + N_HEADS_FOX
ROT_DIM = HEAD_DIM // 4
ROPE_THETA = 500000.0
Q_BLOCK = 128
D_FF = ((8 * D_MODEL // 3 + 255) // 256) * 256
EPS = 1e-6
NEG_INF = -1e30

kernel_name = "hymba_diff_fox_sandwich_block"


def rms_norm(x, g):
    xf = x.astype(jnp.float32)
    y = xf * lax.rsqrt(jnp.mean(xf * xf, axis=-1, keepdims=True) + EPS)
    return (y * g.astype(jnp.float32)).astype(x.dtype)


def rotary_tables(positions, dtype):
    inv_freq = 1.0 / (ROPE_THETA ** (jnp.arange(0, ROT_DIM, 2, dtype=jnp.float32) / ROT_DIM))
    ang = positions.astype(jnp.float32)[..., None] * inv_freq
    cos = jnp.cos(ang)[:, :, None, :].astype(dtype)
    sin = jnp.sin(ang)[:, :, None, :].astype(dtype)
    return cos, sin


def partial_rotary(t, cos, sin):
    half = ROT_DIM // 2
    t1, t2, rest = t[..., :half], t[..., half:ROT_DIM], t[..., ROT_DIM:]
    return jnp.concatenate([t1 * cos - t2 * sin, t2 * cos + t1 * sin, rest], axis=-1)


def diff_attention(q, k, v, lam):
    b, s = q.shape[0], q.shape[1]
    nb = s // Q_BLOCK
    scale = HEAD_DIM ** -0.5
    q_blocks = q.reshape(b, nb, Q_BLOCK, N_HEADS_DIFF, 2, HEAD_DIM).swapaxes(0, 1)
    k_pos = jnp.arange(s)

    def block(args):
        i, qi = args
        q_pos = i * Q_BLOCK + jnp.arange(Q_BLOCK)
        causal = q_pos[:, None] >= k_pos[None, :]
        logits = jnp.einsum('bqhmd,bkhmd->bhmqk', qi, k).astype(jnp.float32) * scale
        logits = jnp.where(causal, logits, NEG_INF)
        p = jax.nn.softmax(logits, axis=-1)
        p = p[:, :, 0] - lam * p[:, :, 1]
        return jnp.einsum('bhqk,bkhe->bqhe', p.astype(v.dtype), v)

    o = lax.map(block, (jnp.arange(nb), q_blocks))
    return o.swapaxes(0, 1).reshape(b, s, N_HEADS_DIFF, 2 * HEAD_DIM)


def forgetting_attention(q, k, v, log_f):
    b, s = q.shape[0], q.shape[1]
    nb = s // Q_BLOCK
    scale = HEAD_DIM ** -0.5
    c = jnp.cumsum(log_f, axis=1)
    c_k = c.transpose(0, 2, 1)[:, :, None, :]
    q_blocks = q.reshape(b, nb, Q_BLOCK, N_HEADS_FOX, HEAD_DIM).swapaxes(0, 1)
    c_blocks = c.reshape(b, nb, Q_BLOCK, N_HEADS_FOX).swapaxes(0, 1)
    k_pos = jnp.arange(s)

    def block(args):
        i, qi, ci = args
        q_pos = i * Q_BLOCK + jnp.arange(Q_BLOCK)
        causal = q_pos[:, None] >= k_pos[None, :]
        decay = ci.transpose(0, 2, 1)[..., None] - c_k
        logits = jnp.einsum('bqhd,bkhd->bhqk', qi, k).astype(jnp.float32) * scale + decay
        logits = jnp.where(causal, logits, NEG_INF)
        p = jax.nn.softmax(logits, axis=-1)
        return jnp.einsum('bhqk,bkhd->bqhd', p.astype(v.dtype), v)

    o = lax.map(block, (jnp.arange(nb), q_blocks, c_blocks))
    return o.swapaxes(0, 1).reshape(b, s, N_HEADS_FOX, HEAD_DIM)


def setup_inputs(seed: int = 0) -> dict:
    key = jax.random.key(seed)
    ks = jax.random.split(key, 16)
    f32 = jnp.float32

    def nrm(k, shape, scale):
        return jax.random.normal(k, shape, f32) * scale

    def gain(k, n):
        return 1.0 + 0.05 * jax.random.normal(k, (DEPTH, n), f32)

    return {
        "x": nrm(ks[0], (BATCH, SEQ, D_MODEL), 1.0),
        "positions": jnp.broadcast_to(jnp.arange(SEQ, dtype=jnp.int32), (BATCH, SEQ)),
        "attn_pre_g": gain(ks[1], D_MODEL),
        "w_in": nrm(ks[2], (DEPTH, D_MODEL, IN_WIDTH), D_MODEL ** -0.5),
        "forget_bias": jax.random.uniform(ks[3], (DEPTH, N_HEADS_FOX), f32, 1.0, 6.0),
        "lam_q1": nrm(ks[4], (DEPTH, HEAD_DIM), 0.1),
        "lam_k1": nrm(ks[5], (DEPTH, HEAD_DIM), 0.1),
        "lam_q2": nrm(ks[6], (DEPTH, HEAD_DIM), 0.1),
        "lam_k2": nrm(ks[7], (DEPTH, HEAD_DIM), 0.1),
        "diff_sub_g": gain(ks[8], 2 * HEAD_DIM),
        "w_out": nrm(ks[9], (DEPTH, MIX_WIDTH, D_MODEL), MIX_WIDTH ** -0.5),
        "attn_post_g": gain(ks[10], D_MODEL),
        "ffn_pre_g": gain(ks[11], D_MODEL),
        "w_gate": nrm(ks[12], (DEPTH, D_MODEL, D_FF), D_MODEL ** -0.5),
        "w_up": nrm(ks[13], (DEPTH, D_MODEL, D_FF), D_MODEL ** -0.5),
        "w_down": nrm(ks[14], (DEPTH, D_FF, D_MODEL), D_FF ** -0.5),
        "ffn_post_g": gain(ks[15], D_MODEL),
    }


def reference(x, positions, attn_pre_g, w_in, forget_bias, lam_q1, lam_k1, lam_q2, lam_k2,
              diff_sub_g, w_out, attn_post_g, ffn_pre_g, w_gate, w_up, w_down, ffn_post_g):
    b, s, _ = x.shape
    f32 = jnp.float32
    cos, sin = rotary_tables(positions, x.dtype)
    split_at = [DIFF_WIDTH, 2 * DIFF_WIDTH, 3 * DIFF_WIDTH,
                3 * DIFF_WIDTH + FOX_WIDTH, 3 * DIFF_WIDTH + 2 * FOX_WIDTH,
                3 * DIFF_WIDTH + 3 * FOX_WIDTH]
    for l in range(DEPTH):
        h = rms_norm(x, attn_pre_g[l])
        proj = h @ w_in[l]
        qa, ka, va, qb, kb, vb, f_logit = jnp.split(proj, split_at, axis=-1)

        qa = partial_rotary(qa.reshape(b, s, 2 * N_HEADS_DIFF, HEAD_DIM), cos, sin)
        ka = partial_rotary(ka.reshape(b, s, 2 * N_HEADS_DIFF, HEAD_DIM), cos, sin)
        qa = qa.reshape(b, s, N_HEADS_DIFF, 2, HEAD_DIM)
        ka = ka.reshape(b, s, N_HEADS_DIFF, 2, HEAD_DIM)
        va = va.reshape(b, s, N_HEADS_DIFF, 2 * HEAD_DIM)
        lam_init = 0.8 - 0.6 * math.exp(-0.3 * l)
        lam = (jnp.exp(jnp.sum(lam_q1[l].astype(f32) * lam_k1[l].astype(f32)))
               - jnp.exp(jnp.sum(lam_q2[l].astype(f32) * lam_k2[l].astype(f32)))
               + lam_init)
        oa = diff_attention(qa, ka, va, lam)
        oa = rms_norm(oa, diff_sub_g[l]) * (1.0 - lam_init)

        log_f = jax.nn.log_sigmoid(f_logit.astype(f32) + forget_bias[l].astype(f32))
        ob = forgetting_attention(qb.reshape(b, s, N_HEADS_FOX, HEAD_DIM),
                                  kb.reshape(b, s, N_HEADS_FOX, HEAD_DIM),
                                  vb.reshape(b, s, N_HEADS_FOX, HEAD_DIM), log_f)

        mixed = jnp.concatenate([oa.reshape(b, s, DIFF_WIDTH), ob.reshape(b, s, FOX_WIDTH)], axis=-1)
        x = x + rms_norm(mixed @ w_out[l], attn_post_g[l])

        h = rms_norm(x, ffn_pre_g[l])
        y = (jax.nn.silu(h @ w_gate[l]) * (h @ w_up[l])) @ w_down[l]
        x = x + rms_norm(y, ffn_post_g[l])
    return x
```

```python
import functools
import math

import jax
import jax.numpy as jnp
from jax import lax
from jax.experimental import pallas as pl
from jax.experimental.pallas import tpu as pltpu

D_MODEL = 1024
HEAD_DIM = 64
HALF = HEAD_DIM
N_HEADS_DIFF = 4
N_HEADS_FOX = 8
DIFF_WIDTH = N_HEADS_DIFF * 2 * HEAD_DIM
FOX_WIDTH = N_HEADS_FOX * HEAD_DIM
ROT_DIM = HEAD_DIM // 4
ROPE_THETA = 500000.0
D_FF = 2816
EPS = 1e-6
NEG_INF = -1e30
SCALE = HEAD_DIM ** -0.5

LANES = 128
IN_PAD = 3200
F_COL = 3 * DIFF_WIDTH + 3 * FOX_WIDTH
VMEM_LIMIT = 56 * 1024 * 1024

F32 = jnp.float32
BF16 = jnp.bfloat16


def _tiles(seq):
    return min(512, seq), min(512, seq)


def _dot(a, b):
    return jnp.dot(a, b, preferred_element_type=F32)


def _dot_nt(a, b):
    return lax.dot_general(a, b, (((1,), (1,)), ((), ())), preferred_element_type=F32)


def _rms(x, g):
    ms = jnp.mean(x * x, axis=-1, keepdims=True)
    return x * lax.rsqrt(ms + EPS) * g


def _split3(c):
    hi = c.astype(BF16).astype(F32)
    r = c - hi
    mid = r.astype(BF16).astype(F32)
    lo = r - mid
    return hi, mid, lo


def _rope_kernel(pos_ref, invf_ref, cos_ref, sin_ref):
    ang = pos_ref[...].astype(F32) * invf_ref[...]
    d = lax.broadcasted_iota(jnp.int32, ang.shape, 1) & (HEAD_DIM - 1)
    c = jnp.cos(ang)
    s = jnp.sin(ang)
    half = ROT_DIM // 2
    cos_ref[...] = jnp.where(d < ROT_DIM, c, 1.0)
    sin_ref[...] = jnp.where(d < half, -s, jnp.where(d < ROT_DIM, s, 0.0))


def _rope_tables(positions, tm):
    b, s = positions.shape
    inv_freq = 1.0 / (ROPE_THETA ** (jnp.arange(0, ROT_DIM, 2, dtype=F32) / ROT_DIM))
    invf = jnp.tile(inv_freq, LANES // (ROT_DIM // 2)).reshape(1, LANES)
    spec = pl.BlockSpec((None, tm, LANES), lambda i, j: (i, j, 0))
    return pl.pallas_call(
        _rope_kernel,
        out_shape=(jax.ShapeDtypeStruct((b, s, LANES), F32),) * 2,
        grid=(b, s // tm),
        in_specs=[pl.BlockSpec((None, tm, 1), lambda i, j: (i, j, 0)),
                  pl.BlockSpec((1, LANES), lambda i, j: (0, 0))],
        out_specs=(spec, spec),
        name="rope_tables",
    )(positions.reshape(b, s, 1), invf)


def _proj_kernel(x_ref, g_ref, w_ref, cos_ref, sin_ref, bias_ref,
                 qd_ref, kd_ref, vd_ref, qf_ref, kf_ref, vf_ref, carry_ref):
    tm = x_ref.shape[0]

    @pl.when(pl.program_id(1) == 0)
    def _():
        carry_ref[...] = jnp.zeros_like(carry_ref)

    h = _rms(x_ref[...], g_ref[...]).astype(BF16)
    cosv = cos_ref[...]
    sinv = sin_ref[...]
    lane = lax.broadcasted_iota(jnp.int32, (tm, LANES), 1)
    d = lane & (HEAD_DIM - 1)
    first_half = lane < HALF

    def rot(y):
        partner = jnp.where(d < ROT_DIM // 2,
                            pltpu.roll(y, LANES - ROT_DIM // 2, 1), pltpu.roll(y, ROT_DIM // 2, 1))
        return y * cosv + partner * sinv

    ya = _dot(h, w_ref[:, 0:DIFF_WIDTH])
    for hd in range(N_HEADS_DIFF):
        qd_ref[hd] = (rot(ya[:, hd * LANES:(hd + 1) * LANES]) * SCALE).astype(BF16)
    ya = _dot(h, w_ref[:, DIFF_WIDTH:2 * DIFF_WIDTH])
    for hd in range(N_HEADS_DIFF):
        kd_ref[hd] = rot(ya[:, hd * LANES:(hd + 1) * LANES]).astype(BF16)
    ya = _dot(h, w_ref[:, 2 * DIFF_WIDTH:3 * DIFF_WIDTH])
    for hd in range(N_HEADS_DIFF):
        vd_ref[hd] = ya[:, hd * LANES:(hd + 1) * LANES].astype(BF16)

    z = _dot(h, w_ref[:, F_COL:F_COL + LANES]) + bias_ref[...]
    log_f = -(jnp.maximum(-z, 0.0) + jnp.log1p(jnp.exp(-jnp.abs(z))))
    log_f = jnp.where(lane < N_HEADS_FOX, log_f, 0.0)
    hi, mid, lo = _split3(log_f)
    packed = (hi + pltpu.roll(mid, N_HEADS_FOX, 1) + pltpu.roll(lo, 2 * N_HEADS_FOX, 1)).astype(BF16)
    tri = (lax.broadcasted_iota(jnp.int32, (tm, tm), 1)
           <= lax.broadcasted_iota(jnp.int32, (tm, tm), 0)).astype(BF16)
    c3 = _dot(tri, packed)
    c = c3 + pltpu.roll(c3, LANES - N_HEADS_FOX, 1) + pltpu.roll(c3, LANES - 2 * N_HEADS_FOX, 1)
    c = jnp.where(lane < N_HEADS_FOX, c + carry_ref[...], 0.0)
    carry_ref[...] = c[tm - 1:tm, :]

    yq = _dot(h, w_ref[:, 3 * DIFF_WIDTH:3 * DIFF_WIDTH + FOX_WIDTH])
    yk = _dot(h, w_ref[:, 3 * DIFF_WIDTH + FOX_WIDTH:3 * DIFF_WIDTH + 2 * FOX_WIDTH])
    yv = _dot(h, w_ref[:, 3 * DIFF_WIDTH + 2 * FOX_WIDTH:3 * DIFF_WIDTH + 3 * FOX_WIDTH])
    for hf in range(N_HEADS_FOX):
        own = first_half if hf % 2 == 0 else jnp.logical_not(first_half)
        tile = slice((hf // 2) * LANES, (hf // 2 + 1) * LANES)
        c_hi, c_mid, c_lo = _split3(jnp.broadcast_to(c[:, hf:hf + 1], (tm, LANES)))
        aug_q = jnp.where(d == 0, c_hi, jnp.where(d == 1, c_mid, jnp.where(d == 2, c_lo,
                          jnp.where(d < 6, 1.0, 0.0))))
        aug_k = jnp.where(d < 3, 1.0, jnp.where(d == 3, -c_hi, jnp.where(d == 4, -c_mid,
                          jnp.where(d == 5, -c_lo, 0.0))))
        qf_ref[hf] = jnp.where(own, yq[:, tile] * SCALE, aug_q).astype(BF16)
        kf_ref[hf] = jnp.where(own, yk[:, tile], aug_k).astype(BF16)
        vf_ref[hf] = jnp.where(own, yv[:, tile], 1.0).astype(BF16)


def _project(x, g, w, cos, sin, bias, tm):
    b, s, _ = x.shape
    tok = lambda i, j: (i, j, 0)
    const = lambda i, j: (0, 0)
    head_out = lambda n: pl.BlockSpec((None, n, tm, LANES), lambda i, j: (i, 0, j, 0))
    head_shape = lambda n: jax.ShapeDtypeStruct((b, n, s, LANES), BF16)
    return pl.pallas_call(
        _proj_kernel,
        out_shape=(head_shape(N_HEADS_DIFF),) * 3 + (head_shape(N_HEADS_FOX),) * 3,
        grid=(b, s // tm),
        in_specs=[pl.BlockSpec((None, tm, D_MODEL), tok),
                  pl.BlockSpec((1, D_MODEL), const),
                  pl.BlockSpec((D_MODEL, IN_PAD), const, pipeline_mode=pl.Buffered(1)),
                  pl.BlockSpec((None, tm, LANES), tok),
                  pl.BlockSpec((None, tm, LANES), tok),
                  pl.BlockSpec((1, LANES), const)],
        out_specs=(head_out(N_HEADS_DIFF),) * 3 + (head_out(N_HEADS_FOX),) * 3,
        scratch_shapes=[pltpu.VMEM((1, LANES), F32)],
        compiler_params=pltpu.CompilerParams(
            dimension_semantics=("arbitrary", "arbitrary"), vmem_limit_bytes=VMEM_LIMIT),
        name="in_proj",
    )(x, g, w, cos, sin, bias)


def _online_softmax_step(s, v, m_ref, acc_ref, l_ref=None):
    tq, tk = s.shape
    reps = tk // LANES
    m_prev = m_ref[...]
    m_new = jnp.maximum(m_prev, jnp.max(s, axis=1, keepdims=True))
    alpha = jnp.exp(m_prev - m_new)
    p = jnp.exp(s - jnp.tile(m_new, (1, reps)))
    if l_ref is not None:
        psum = p[:, 0:LANES]
        for r in range(1, reps):
            psum = psum + p[:, r * LANES:(r + 1) * LANES]
        l_ref[...] = alpha * l_ref[...] + psum
    acc_ref[...] = alpha * acc_ref[...] + _dot(p.astype(BF16), v)
    m_ref[...] = m_new


def _causal_mask(s):
    row = lax.broadcasted_iota(jnp.int32, s.shape, 0)
    col = lax.broadcasted_iota(jnp.int32, s.shape, 1)
    return jnp.where(row >= col, s, NEG_INF)


def _diff_attn_kernel(lam_ref, g_ref, q_ref, k_ref, v_ref, o_ref, m_sc, l_sc, acc_sc, *, lam_init):
    t = q_ref.shape[0]
    qi = pl.program_id(2)
    q = q_ref[...]
    lane = lax.broadcasted_iota(jnp.int32, q.shape, 1)
    zero = jnp.zeros_like(q)
    qs = (jnp.where(lane < HALF, q, zero), jnp.where(lane >= HALF, q, zero))
    m_sc[...] = jnp.full_like(m_sc, NEG_INF)
    l_sc[...] = jnp.zeros_like(l_sc)
    acc_sc[...] = jnp.zeros_like(acc_sc)

    def step(j, masked):
        start = pl.multiple_of(j * t, t)
        k = k_ref[pl.ds(start, t), :]
        v = v_ref[pl.ds(start, t), :]
        for mp in range(2):
            s = _dot_nt(qs[mp], k)
            if masked:
                s = _causal_mask(s)
            _online_softmax_step(s, v, m_sc.at[mp], acc_sc.at[mp], l_sc.at[mp])

    def body(j, carry):
        step(j, False)
        return carry

    lax.fori_loop(0, qi, body, 0)
    step(qi, True)

    lam = (jnp.exp(jnp.sum(lam_ref[0:1, :] * lam_ref[1:2, :], axis=1, keepdims=True))
           - jnp.exp(jnp.sum(lam_ref[2:3, :] * lam_ref[3:4, :], axis=1, keepdims=True)) + lam_init)
    o0 = acc_sc[0] / jnp.sum(l_sc[0], axis=1, keepdims=True)
    o1 = acc_sc[1] / jnp.sum(l_sc[1], axis=1, keepdims=True)
    o = o0 - lam * o1
    o_ref[...] = (_rms(o, g_ref[...]) * (1.0 - lam_init)).astype(o_ref.dtype)


def _diff_attention(lam_vecs, g, qd, kd, vd, t, lam_init):
    b, nh, s, _ = qd.shape
    kv_spec = pl.BlockSpec((None, None, s, LANES), lambda i, h, j: (i, h, 0, 0))
    return pl.pallas_call(
        functools.partial(_diff_attn_kernel, lam_init=lam_init),
        out_shape=jax.ShapeDtypeStruct((b, s, DIFF_WIDTH), BF16),
        grid=(b, nh, s // t),
        in_specs=[pl.BlockSpec((8, LANES), lambda i, h, j: (0, 0)),
                  pl.BlockSpec((1, LANES), lambda i, h, j: (0, 0)),
                  pl.BlockSpec((None, None, t, LANES), lambda i, h, j: (i, h, j, 0)),
                  kv_spec, kv_spec],
        out_specs=pl.BlockSpec((None, t, LANES), lambda i, h, j: (i, j, h)),
        scratch_shapes=[pltpu.VMEM((2, t, LANES), F32)] * 3,
        compiler_params=pltpu.CompilerParams(
            dimension_semantics=("arbitrary",) * 3, vmem_limit_bytes=VMEM_LIMIT),
        name="diff_attention",
    )(lam_vecs, g, qd, kd, vd)


def _fox_attn_kernel(q_ref, k_ref, v_ref, o_ref, m_sc, acc_sc):
    t = q_ref.shape[1]
    qi = pl.program_id(2)
    m_sc[...] = jnp.full_like(m_sc, NEG_INF)
    acc_sc[...] = jnp.zeros_like(acc_sc)

    def step(j, masked):
        start = pl.multiple_of(j * t, t)
        for hh in range(2):
            s = _dot_nt(q_ref[hh], k_ref[hh, pl.ds(start, t), :])
            if masked:
                s = _causal_mask(s)
            _online_softmax_step(s, v_ref[hh, pl.ds(start, t), :], m_sc.at[hh], acc_sc.at[hh])

    def body(j, carry):
        step(j, False)
        return carry

    lax.fori_loop(0, qi, body, 0)
    step(qi, True)

    a0 = acc_sc[0]
    a1 = acc_sc[1]
    lane = lax.broadcasted_iota(jnp.int32, a0.shape, 1)
    o = jnp.where(lane < HALF, a0 / pltpu.roll(a0, HALF, 1), a1 / pltpu.roll(a1, HALF, 1))
    o_ref[...] = o.astype(o_ref.dtype)


def _fox_attention(qf, kf, vf, t):
    b, nh, s, _ = qf.shape
    kv_spec = pl.BlockSpec((None, 2, s, LANES), lambda i, h, j: (i, h, 0, 0))
    return pl.pallas_call(
        _fox_attn_kernel,
        out_shape=jax.ShapeDtypeStruct((b, s, FOX_WIDTH), BF16),
        grid=(b, nh // 2, s // t),
        in_specs=[pl.BlockSpec((None, 2, t, LANES), lambda i, h, j: (i, h, j, 0)), kv_spec, kv_spec],
        out_specs=pl.BlockSpec((None, t, LANES), lambda i, h, j: (i, j, h)),
        scratch_shapes=[pltpu.VMEM((2, t, LANES), F32)] * 2,
        compiler_params=pltpu.CompilerParams(
            dimension_semantics=("arbitrary",) * 3, vmem_limit_bytes=VMEM_LIMIT),
        name="fox_attention",
    )(qf, kf, vf)


def _outproj_kernel(oa_ref, ob_ref, x_ref, wa_ref, wb_ref, g_ref, o_ref):
    y = _dot(oa_ref[...], wa_ref[...]) + _dot(ob_ref[...], wb_ref[...])
    o_ref[...] = x_ref[...] + _rms(y, g_ref[...])


def _out_project(oa, ob, x, wa, wb, g, tm):
    b, s, _ = x.shape
    tok = lambda i, j: (i, j, 0)
    const = lambda i, j: (0, 0)
    return pl.pallas_call(
        _outproj_kernel,
        out_shape=jax.ShapeDtypeStruct(x.shape, F32),
        grid=(b, s // tm),
        in_specs=[pl.BlockSpec((None, tm, DIFF_WIDTH), tok),
                  pl.BlockSpec((None, tm, FOX_WIDTH), tok),
                  pl.BlockSpec((None, tm, D_MODEL), tok),
                  pl.BlockSpec((DIFF_WIDTH, D_MODEL), const, pipeline_mode=pl.Buffered(1)),
                  pl.BlockSpec((FOX_WIDTH, D_MODEL), const, pipeline_mode=pl.Buffered(1)),
                  pl.BlockSpec((1, D_MODEL), const)],
        out_specs=pl.BlockSpec((None, tm, D_MODEL), tok),
        compiler_params=pltpu.CompilerParams(
            dimension_semantics=("arbitrary", "arbitrary"), vmem_limit_bytes=VMEM_LIMIT),
        name="out_proj",
    )(oa, ob, x, wa, wb, g)


FF_CHUNK = D_FF // 2


def _ffn_kernel(x_ref, gpre_ref, wg_ref, wu_ref, wd_ref, gpost_ref, o_ref):
    x = x_ref[...]
    h = _rms(x, gpre_ref[...]).astype(BF16)
    y = None
    for c0 in range(0, D_FF, FF_CHUNK):
        gate = _dot(h, wg_ref[:, c0:c0 + FF_CHUNK])
        up = _dot(h, wu_ref[:, c0:c0 + FF_CHUNK])
        act = (gate * jax.nn.sigmoid(gate) * up).astype(BF16)
        part = _dot(act, wd_ref[c0:c0 + FF_CHUNK, :])
        y = part if y is None else y + part
    o_ref[...] = x + _rms(y, gpost_ref[...])


def _ffn(x, gpre, wg, wu, wd, gpost, tm):
    b, s, _ = x.shape
    tok = lambda i, j: (i, j, 0)
    const = lambda i, j: (0, 0)
    return pl.pallas_call(
        _ffn_kernel,
        out_shape=jax.ShapeDtypeStruct(x.shape, F32),
        grid=(b, s // tm),
        in_specs=[pl.BlockSpec((None, tm, D_MODEL), tok),
                  pl.BlockSpec((1, D_MODEL), const),
                  pl.BlockSpec((D_MODEL, D_FF), const, pipeline_mode=pl.Buffered(1)),
                  pl.BlockSpec((D_MODEL, D_FF), const, pipeline_mode=pl.Buffered(1)),
                  pl.BlockSpec((D_FF, D_MODEL), const, pipeline_mode=pl.Buffered(1)),
                  pl.BlockSpec((1, D_MODEL), const)],
        out_specs=pl.BlockSpec((None, tm, D_MODEL), tok),
        compiler_params=pltpu.CompilerParams(
            dimension_semantics=("arbitrary", "arbitrary"), vmem_limit_bytes=VMEM_LIMIT),
        name="swiglu_ffn",
    )(x, gpre, wg, wu, wd, gpost)


def kernel(x, positions, attn_pre_g, w_in, forget_bias, lam_q1, lam_k1, lam_q2, lam_k2,
           diff_sub_g, w_out, attn_post_g, ffn_pre_g, w_gate, w_up, w_down, ffn_post_g):
    depth = w_in.shape[0]
    seq = x.shape[1]
    tm, t = _tiles(seq)
    assert seq % tm == 0 and seq % t == 0 and x.shape[2] == D_MODEL

    cos, sin = _rope_tables(positions, tm)
    w_in_b = jnp.pad(w_in, ((0, 0), (0, 0), (0, IN_PAD - w_in.shape[2]))).astype(BF16)
    w_out_b = w_out.astype(BF16)
    w_gate_b = w_gate.astype(BF16)
    w_up_b = w_up.astype(BF16)
    w_down_b = w_down.astype(BF16)
    bias = jnp.pad(forget_bias.astype(F32), ((0, 0), (0, LANES - N_HEADS_FOX)))
    lam_vecs = jnp.pad(jnp.stack([lam_q1, lam_k1, lam_q2, lam_k2], axis=1).astype(F32),
                       ((0, 0), (0, 4), (0, LANES - HEAD_DIM)))
    row = lambda v: v.reshape(1, -1).astype(F32)

    for l in range(depth):
        lam_init = 0.8 - 0.6 * math.exp(-0.3 * l)
        qd, kd, vd, qf, kf, vf = _project(x, row(attn_pre_g[l]), w_in_b[l], cos, sin,
                                          bias[l:l + 1], tm)
        oa = _diff_attention(lam_vecs[l], row(diff_sub_g[l]), qd, kd, vd, t, lam_init)
        ob = _fox_attention(qf, kf, vf, t)
        x = _out_project(oa, ob, x, w_out_b[l, :DIFF_WIDTH], w_out_b[l, DIFF_WIDTH:],
                         row(attn_post_g[l]), tm)
        x = _ffn(x, row(ffn_pre_g[l]), w_gate_b[l], w_up_b[l], w_down_b[l], row(ffn_post_g[l]), tm)
    return x
```

```python
import functools
import math

import jax
import jax.numpy as jnp
from jax import lax
from jax.experimental import pallas as pl
from jax.experimental.pallas import tpu as pltpu

D_MODEL = 1024
HEAD_DIM = 64
N_HEADS_DIFF = 4
N_HEADS_FOX = 8
DIFF_WIDTH = N_HEADS_DIFF * 2 * HEAD_DIM
FOX_WIDTH = N_HEADS_FOX * HEAD_DIM
ROT_DIM = HEAD_DIM // 4
ROT_HALF = ROT_DIM // 2
ROPE_THETA = 500000.0
D_FF = 2816
EPS = 1e-6
NEG_INF = -1e30
SCALE = HEAD_DIM ** -0.5

LANES = 128
SUBLANES = 8
BF16_ROWS = 16
F_ROW = 3 * DIFF_WIDTH + 3 * FOX_WIDTH
IN_ROWS = F_ROW + BF16_ROWS
V_DIFF_ROWS = 2 * HEAD_DIM + BF16_ROWS
V_FOX_ROWS = HEAD_DIM + BF16_ROWS
VMEM_LIMIT = 56 * 1024 * 1024

F32 = jnp.float32
BF16 = jnp.bfloat16


def _tile(seq):
    return min(512, seq)


def _dot(a, b):
    return jnp.dot(a, b, preferred_element_type=F32)


def _dot_nt(a, b):
    return lax.dot_general(a, b, (((1,), (1,)), ((), ())), preferred_element_type=F32)


def _rms(x, g):
    ms = jnp.mean(x * x, axis=-1, keepdims=True)
    return x * lax.rsqrt(ms + EPS) * g


def _split3(c):
    hi = c.astype(BF16).astype(F32)
    r = c - hi
    mid = r.astype(BF16).astype(F32)
    lo = r - mid
    return hi, mid, lo


def _rope_kernel(pos_ref, invf_ref, cos_ref, sin_ref):
    ang = invf_ref[...] * pos_ref[...].astype(F32)
    cos_ref[...] = jnp.cos(ang)
    sin_ref[...] = jnp.sin(ang)


def _rope_tables(positions, tm):
    b, s = positions.shape
    inv_freq = 1.0 / (ROPE_THETA ** (jnp.arange(0, ROT_DIM, 2, dtype=F32) / ROT_DIM))
    spec = pl.BlockSpec((None, ROT_HALF, tm), lambda i, j: (i, 0, j))
    return pl.pallas_call(
        _rope_kernel,
        out_shape=(jax.ShapeDtypeStruct((b, ROT_HALF, s), F32),) * 2,
        grid=(b, s // tm),
        in_specs=[pl.BlockSpec((None, 1, tm), lambda i, j: (i, 0, j)),
                  pl.BlockSpec((ROT_HALF, 1), lambda i, j: (0, 0))],
        out_specs=(spec, spec),
        name="rope_tables",
    )(positions.reshape(b, 1, s), inv_freq.reshape(ROT_HALF, 1))


def _proj_kernel(x_ref, g_ref, wt_ref, cos_ref, sin_ref, bias_ref,
                 qd_ref, kd_ref, vd_ref, qf_ref, kf_ref, vf_ref, carry_ref):
    tm = x_ref.shape[0]

    @pl.when(pl.program_id(1) == 0)
    def _():
        carry_ref[...] = jnp.zeros_like(carry_ref)

    h = _rms(x_ref[...], g_ref[...]).astype(BF16)
    cosv = cos_ref[...]
    sinv = sin_ref[...]
    ones_blk = jnp.ones((BF16_ROWS, tm), F32)

    def seg(r0, n):
        return _dot_nt(wt_ref[r0:r0 + n, :], h)

    def rot(y):
        t1 = y[0:ROT_HALF]
        t2 = y[ROT_HALF:ROT_DIM]
        return jnp.concatenate([t1 * cosv - t2 * sinv, t2 * cosv + t1 * sinv, y[ROT_DIM:]], axis=0)

    def rot_head(y):
        return jnp.concatenate([rot(y[0:HEAD_DIM]), rot(y[HEAD_DIM:])], axis=0)

    y = seg(0, DIFF_WIDTH)
    for hd in range(N_HEADS_DIFF):
        qd_ref[hd] = (rot_head(y[hd * LANES:(hd + 1) * LANES]) * SCALE).astype(BF16)
    y = seg(DIFF_WIDTH, DIFF_WIDTH)
    for hd in range(N_HEADS_DIFF):
        kd_ref[hd] = rot_head(y[hd * LANES:(hd + 1) * LANES]).T.astype(BF16)
    y = seg(2 * DIFF_WIDTH, DIFF_WIDTH)
    for hd in range(N_HEADS_DIFF):
        vd_ref[hd] = jnp.concatenate([y[hd * LANES:(hd + 1) * LANES], ones_blk], axis=0).astype(BF16)

    row16 = lax.broadcasted_iota(jnp.int32, (BF16_ROWS, tm), 0)
    z = seg(F_ROW, BF16_ROWS) + bias_ref[...]
    log_f = -(jnp.maximum(-z, 0.0) + jnp.log1p(jnp.exp(-jnp.abs(z))))
    log_f = jnp.where(row16 < N_HEADS_FOX, log_f, 0.0)
    packed = jnp.concatenate(_split3(log_f), axis=0).astype(BF16)
    tri = (lax.broadcasted_iota(jnp.int32, (tm, tm), 0)
           <= lax.broadcasted_iota(jnp.int32, (tm, tm), 1)).astype(BF16)
    c3 = _dot(packed, tri)
    c = (c3[0:BF16_ROWS] + c3[BF16_ROWS:2 * BF16_ROWS] + c3[2 * BF16_ROWS:]
         + jnp.tile(carry_ref[...], (1, tm // LANES)))
    carry_ref[...] = jnp.broadcast_to(c[:, tm - 1:tm], carry_ref.shape)

    yq = seg(3 * DIFF_WIDTH, FOX_WIDTH)
    yk = seg(3 * DIFF_WIDTH + FOX_WIDTH, FOX_WIDTH)
    yv = seg(3 * DIFF_WIDTH + 2 * FOX_WIDTH, FOX_WIDTH)
    row8 = lax.broadcasted_iota(jnp.int32, (SUBLANES, tm), 0)
    zpad = jnp.zeros((LANES - HEAD_DIM - SUBLANES, tm), F32)
    for hf in range(N_HEADS_FOX):
        c_hi, c_mid, c_lo = _split3(jnp.broadcast_to(c[hf:hf + 1, :], (SUBLANES, tm)))
        aug_q = jnp.where(row8 == 0, c_hi, jnp.where(row8 == 1, c_mid, jnp.where(row8 == 2, c_lo,
                          jnp.where(row8 < 6, 1.0, 0.0))))
        aug_k = jnp.where(row8 < 3, 1.0, jnp.where(row8 == 3, -c_hi, jnp.where(row8 == 4, -c_mid,
                          jnp.where(row8 == 5, -c_lo, 0.0))))
        feat = slice(hf * HEAD_DIM, (hf + 1) * HEAD_DIM)
        qf_ref[hf] = jnp.concatenate([yq[feat] * SCALE, aug_q, zpad], axis=0).astype(BF16)
        kf_ref[hf] = jnp.concatenate([yk[feat], aug_k, zpad], axis=0).T.astype(BF16)
        vf_ref[hf] = jnp.concatenate([yv[feat], ones_blk], axis=0).astype(BF16)


def _project(x, g, wt, cos, sin, bias, tm):
    b, s, _ = x.shape
    nt = s // tm
    tok = lambda i, j: (i, j, 0)
    const = lambda i, j: (0, 0)
    rope = pl.BlockSpec((None, ROT_HALF, tm), lambda i, j: (i, 0, j))
    sd = jax.ShapeDtypeStruct

    def specs(n, v_rows):
        return (
            (sd((b, n, LANES, s), BF16), pl.BlockSpec((None, n, LANES, tm), lambda i, j: (i, 0, 0, j))),
            (sd((b, n, s, LANES), BF16), pl.BlockSpec((None, n, tm, LANES), lambda i, j: (i, 0, j, 0))),
            (sd((b, n, nt, v_rows, tm), BF16),
             pl.BlockSpec((None, n, None, v_rows, tm), lambda i, j: (i, 0, j, 0, 0))),
        )

    outs = specs(N_HEADS_DIFF, V_DIFF_ROWS) + specs(N_HEADS_FOX, V_FOX_ROWS)
    return pl.pallas_call(
        _proj_kernel,
        out_shape=tuple(o[0] for o in outs),
        grid=(b, nt),
        in_specs=[pl.BlockSpec((None, tm, D_MODEL), tok),
                  pl.BlockSpec((1, D_MODEL), const),
                  pl.BlockSpec((IN_ROWS, D_MODEL), const, pipeline_mode=pl.Buffered(1)),
                  rope, rope,
                  pl.BlockSpec((BF16_ROWS, 1), const)],
        out_specs=tuple(o[1] for o in outs),
        scratch_shapes=[pltpu.VMEM((BF16_ROWS, LANES), F32)],
        compiler_params=pltpu.CompilerParams(
            dimension_semantics=("arbitrary", "arbitrary"), vmem_limit_bytes=VMEM_LIMIT),
        name="in_proj",
    )(x, g, wt, cos, sin, bias)


def _causal_mask(s_t):
    key = lax.broadcasted_iota(jnp.int32, s_t.shape, 0)
    query = lax.broadcasted_iota(jnp.int32, s_t.shape, 1)
    return jnp.where(query >= key, s_t, NEG_INF)


def _attend(qi, q_ts, k_at, v_at, s_buf, mc_buf, p_buf, al_buf, m_sc, acc_sc):
    n = len(q_ts)
    m_sc[...] = jnp.full_like(m_sc, NEG_INF)
    acc_sc[...] = jnp.zeros_like(acc_sc)

    def stage_a(par, blk, masked=False):
        for i in range(n):
            s_t = _dot(k_at(i, blk), q_ts[i]())
            if masked:
                s_t = _causal_mask(s_t)
            s_buf[par, i] = s_t
            mc_buf[par, i] = jnp.max(s_t, axis=0, keepdims=True)

    def stage_b(par):
        for i in range(n):
            m_prev = m_sc[i]
            m_new = jnp.maximum(m_prev, mc_buf[par, i])
            al_buf[par, i] = jnp.exp(m_prev - m_new)
            p_buf[par, i] = jnp.exp(s_buf[par, i] - m_new).astype(BF16)
            m_sc[i] = m_new

    def stage_c(par, blk):
        for i in range(n):
            acc_sc[i] = al_buf[par, i] * acc_sc[i] + _dot(v_at(i, blk), p_buf[par, i])

    stage_a(0, qi, masked=True)

    @pl.when(qi == 0)
    def _():
        stage_b(0)
        stage_c(0, qi)

    @pl.when(qi > 0)
    def _():
        stage_a(1, 0)
        stage_b(0)
        n_steady = qi - 1

        def pair(ii, carry):
            i = 1 + 2 * ii
            stage_a(0, i)
            stage_b(1)
            stage_c(0, jnp.where(ii == 0, qi, i - 2))
            stage_a(1, i + 1)
            stage_b(0)
            stage_c(1, i - 1)
            return carry

        lax.fori_loop(0, n_steady // 2, pair, 0)

        @pl.when(n_steady % 2 == 1)
        def _():
            i = qi - 1
            stage_a(0, i)
            stage_b(1)
            stage_c(0, jnp.where(i == 1, qi, i - 2))

        @pl.when(qi % 2 == 0)
        def _():
            stage_b(0)
            stage_c(1, qi - 2)
            stage_c(0, qi - 1)

        @pl.when(qi % 2 == 1)
        def _():
            stage_b(1)
            stage_c(0, jnp.where(qi == 1, qi, qi - 2))
            stage_c(1, qi - 1)


def _attend_scratch(n, v_rows, t):
    return [pltpu.VMEM((2, n, t, t), F32), pltpu.VMEM((2, n, 1, t), F32),
            pltpu.VMEM((2, n, t, t), BF16), pltpu.VMEM((2, n, 1, t), F32),
            pltpu.VMEM((n, 1, t), F32), pltpu.VMEM((n, v_rows, t), F32)]


def _diff_attn_kernel(lam_ref, g_ref, q_ref, k_ref, v_ref, o_ref,
                      s_buf, mc_buf, p_buf, al_buf, m_sc, acc_sc, *, lam_init):
    t = q_ref.shape[1]
    qi = pl.program_id(2)
    zero = jnp.zeros((HEAD_DIM, t), BF16)
    q_ts = (lambda: jnp.concatenate([q_ref[0:HEAD_DIM, :], zero], axis=0),
            lambda: jnp.concatenate([zero, q_ref[HEAD_DIM:, :]], axis=0))
    k_at = lambda i, j: k_ref[pl.ds(pl.multiple_of(j * t, t), t), :]
    v_at = lambda i, j: v_ref[j]
    _attend(qi, q_ts, k_at, v_at, s_buf, mc_buf, p_buf, al_buf, m_sc, acc_sc)

    lam = (jnp.exp(jnp.sum(lam_ref[0:1, :] * lam_ref[1:2, :], axis=1, keepdims=True))
           - jnp.exp(jnp.sum(lam_ref[2:3, :] * lam_ref[3:4, :], axis=1, keepdims=True)) + lam_init)
    nv = 2 * HEAD_DIM
    a0 = acc_sc[0]
    a1 = acc_sc[1]
    o_t = a0[0:nv] / a0[nv:nv + 1] - lam * (a1[0:nv] / a1[nv:nv + 1])
    ms = jnp.mean(o_t * o_t, axis=0, keepdims=True)
    y_t = o_t * lax.rsqrt(ms + EPS) * g_ref[...] * (1.0 - lam_init)
    o_ref[...] = y_t.T.astype(o_ref.dtype)


def _diff_attention(lam_vecs, g_col, qd, kd, vd, t, lam_init):
    b, nh, s, _ = kd.shape
    nkb = s // t
    return pl.pallas_call(
        functools.partial(_diff_attn_kernel, lam_init=lam_init),
        out_shape=jax.ShapeDtypeStruct((b, s, DIFF_WIDTH), BF16),
        grid=(b, nh, nkb),
        in_specs=[pl.BlockSpec((SUBLANES, LANES), lambda i, h, j: (0, 0)),
                  pl.BlockSpec((LANES, 1), lambda i, h, j: (0, 0)),
                  pl.BlockSpec((None, None, LANES, t), lambda i, h, j: (i, h, 0, j)),
                  pl.BlockSpec((None, None, s, LANES), lambda i, h, j: (i, h, 0, 0)),
                  pl.BlockSpec((None, None, nkb, V_DIFF_ROWS, t), lambda i, h, j: (i, h, 0, 0, 0))],
        out_specs=pl.BlockSpec((None, t, LANES), lambda i, h, j: (i, j, h)),
        scratch_shapes=_attend_scratch(2, V_DIFF_ROWS, t),
        compiler_params=pltpu.CompilerParams(
            dimension_semantics=("arbitrary",) * 3, vmem_limit_bytes=VMEM_LIMIT),
        name="diff_attention",
    )(lam_vecs, g_col, qd, kd, vd)


def _fox_attn_kernel(q_ref, k_ref, v_ref, o_ref, s_buf, mc_buf, p_buf, al_buf, m_sc, acc_sc):
    t = q_ref.shape[2]
    qi = pl.program_id(2)
    q_ts = tuple((lambda hh=hh: q_ref[hh]) for hh in range(2))
    k_at = lambda i, j: k_ref[i, pl.ds(pl.multiple_of(j * t, t), t), :]
    v_at = lambda i, j: v_ref[i, j]
    _attend(qi, q_ts, k_at, v_at, s_buf, mc_buf, p_buf, al_buf, m_sc, acc_sc)

    outs = []
    for hh in range(2):
        a = acc_sc[hh]
        outs.append(a[0:HEAD_DIM] / a[HEAD_DIM:HEAD_DIM + 1])
    o_ref[...] = jnp.concatenate(outs, axis=0).T.astype(o_ref.dtype)


def _fox_attention(qf, kf, vf, t):
    b, nh, s, _ = kf.shape
    nkb = s // t
    return pl.pallas_call(
        _fox_attn_kernel,
        out_shape=jax.ShapeDtypeStruct((b, s, FOX_WIDTH), BF16),
        grid=(b, nh // 2, nkb),
        in_specs=[pl.BlockSpec((None, 2, LANES, t), lambda i, h, j: (i, h, 0, j)),
                  pl.BlockSpec((None, 2, s, LANES), lambda i, h, j: (i, h, 0, 0)),
                  pl.BlockSpec((None, 2, nkb, V_FOX_ROWS, t), lambda i, h, j: (i, h, 0, 0, 0))],
        out_specs=pl.BlockSpec((None, t, LANES), lambda i, h, j: (i, j, h)),
        scratch_shapes=_attend_scratch(2, V_FOX_ROWS, t),
        compiler_params=pltpu.CompilerParams(
            dimension_semantics=("arbitrary",) * 3, vmem_limit_bytes=VMEM_LIMIT),
        name="fox_attention",
    )(qf, kf, vf)


def _outproj_kernel(oa_ref, ob_ref, x_ref, wa_ref, wb_ref, g_ref, o_ref):
    y = _dot(oa_ref[...], wa_ref[...]) + _dot(ob_ref[...], wb_ref[...])
    o_ref[...] = x_ref[...] + _rms(y, g_ref[...])


def _out_project(oa, ob, x, wa, wb, g, tm):
    b, s, _ = x.shape
    tok = lambda i, j: (i, j, 0)
    const = lambda i, j: (0, 0)
    return pl.pallas_call(
        _outproj_kernel,
        out_shape=jax.ShapeDtypeStruct(x.shape, F32),
        grid=(b, s // tm),
        in_specs=[pl.BlockSpec((None, tm, DIFF_WIDTH), tok),
                  pl.BlockSpec((None, tm, FOX_WIDTH), tok),
                  pl.BlockSpec((None, tm, D_MODEL), tok),
                  pl.BlockSpec((DIFF_WIDTH, D_MODEL), const, pipeline_mode=pl.Buffered(1)),
                  pl.BlockSpec((FOX_WIDTH, D_MODEL), const, pipeline_mode=pl.Buffered(1)),
                  pl.BlockSpec((1, D_MODEL), const)],
        out_specs=pl.BlockSpec((None, tm, D_MODEL), tok),
        compiler_params=pltpu.CompilerParams(
            dimension_semantics=("arbitrary", "arbitrary"), vmem_limit_bytes=VMEM_LIMIT),
        name="out_proj",
    )(oa, ob, x, wa, wb, g)


FF_CHUNK = D_FF // 2


def _ffn_kernel(x_ref, gpre_ref, wg_ref, wu_ref, wd_ref, gpost_ref, o_ref):
    x = x_ref[...]
    h = _rms(x, gpre_ref[...]).astype(BF16)
    y = None
    for c0 in range(0, D_FF, FF_CHUNK):
        gate = _dot(h, wg_ref[:, c0:c0 + FF_CHUNK])
        up = _dot(h, wu_ref[:, c0:c0 + FF_CHUNK])
        act = (gate * jax.nn.sigmoid(gate) * up).astype(BF16)
        part = _dot(act, wd_ref[c0:c0 + FF_CHUNK, :])
        y = part if y is None else y + part
    o_ref[...] = x + _rms(y, gpost_ref[...])


def _ffn(x, gpre, wg, wu, wd, gpost, tm):
    b, s, _ = x.shape
    tok = lambda i, j: (i, j, 0)
    const = lambda i, j: (0, 0)
    return pl.pallas_call(
        _ffn_kernel,
        out_shape=jax.ShapeDtypeStruct(x.shape, F32),
        grid=(b, s // tm),
        in_specs=[pl.BlockSpec((None, tm, D_MODEL), tok),
                  pl.BlockSpec((1, D_MODEL), const),
                  pl.BlockSpec((D_MODEL, D_FF), const, pipeline_mode=pl.Buffered(1)),
                  pl.BlockSpec((D_MODEL, D_FF), const, pipeline_mode=pl.Buffered(1)),
                  pl.BlockSpec((D_FF, D_MODEL), const, pipeline_mode=pl.Buffered(1)),
                  pl.BlockSpec((1, D_MODEL), const)],
        out_specs=pl.BlockSpec((None, tm, D_MODEL), tok),
        compiler_params=pltpu.CompilerParams(
            dimension_semantics=("arbitrary", "arbitrary"), vmem_limit_bytes=VMEM_LIMIT),
        name="swiglu_ffn",
    )(x, gpre, wg, wu, wd, gpost)


def kernel(x, positions, attn_pre_g, w_in, forget_bias, lam_q1, lam_k1, lam_q2, lam_k2,
           diff_sub_g, w_out, attn_post_g, ffn_pre_g, w_gate, w_up, w_down, ffn_post_g):
    depth = w_in.shape[0]
    seq = x.shape[1]
    t = _tile(seq)
    assert seq % t == 0 and t % LANES == 0 and x.shape[2] == D_MODEL

    cos, sin = _rope_tables(positions, t)
    w_in_t = jnp.pad(jnp.swapaxes(w_in, 1, 2),
                     ((0, 0), (0, IN_ROWS - w_in.shape[2]), (0, 0))).astype(BF16)
    w_out_b = w_out.astype(BF16)
    w_gate_b = w_gate.astype(BF16)
    w_up_b = w_up.astype(BF16)
    w_down_b = w_down.astype(BF16)
    bias = jnp.pad(forget_bias.astype(F32), ((0, 0), (0, BF16_ROWS - N_HEADS_FOX)))[:, :, None]
    lam_vecs = jnp.pad(jnp.stack([lam_q1, lam_k1, lam_q2, lam_k2], axis=1).astype(F32),
                       ((0, 0), (0, SUBLANES - 4), (0, LANES - HEAD_DIM)))
    row = lambda v: v.reshape(1, -1).astype(F32)

    for l in range(depth):
        lam_init = 0.8 - 0.6 * math.exp(-0.3 * l)
        qd, kd, vd, qf, kf, vf = _project(x, row(attn_pre_g[l]), w_in_t[l], cos, sin, bias[l], t)
        oa = _diff_attention(lam_vecs[l], diff_sub_g[l].reshape(-1, 1).astype(F32), qd, kd, vd,
                             t, lam_init)
        ob = _fox_attention(qf, kf, vf, t)
        x = _out_project(oa, ob, x, w_out_b[l, :DIFF_WIDTH], w_out_b[l, DIFF_WIDTH:],
                         row(attn_post_g[l]), t)
        x = _ffn(x, row(ffn_pre_g[l]), w_gate_b[l], w_up_b[l], w_down_b[l], row(ffn_post_g[l]), t)
    return x
```

```python
import functools
import math

import jax
import jax.numpy as jnp
from jax import lax
from jax.experimental import pallas as pl
from jax.experimental.pallas import tpu as pltpu

D_MODEL = 1024
HEAD_DIM = 64
N_HEADS_DIFF = 4
N_HEADS_FOX = 8
DIFF_WIDTH = N_HEADS_DIFF * 2 * HEAD_DIM
FOX_WIDTH = N_HEADS_FOX * HEAD_DIM
ROT_DIM = HEAD_DIM // 4
ROT_HALF = ROT_DIM // 2
ROPE_THETA = 500000.0
D_FF = 2816
EPS = 1e-6
NEG_INF = -1e30
LOG2E = 1.4426950408889634
Q_SCALE = HEAD_DIM ** -0.5 * LOG2E

LANES = 128
SUBLANES = 8
BF16_ROWS = 16
F_ROW = 3 * DIFF_WIDTH + 3 * FOX_WIDTH
IN_ROWS = F_ROW + BF16_ROWS
V_DIFF_ROWS = 2 * HEAD_DIM + BF16_ROWS
V_FOX_ROWS = HEAD_DIM + BF16_ROWS
VMEM_LIMIT = 56 * 1024 * 1024

F32 = jnp.float32
BF16 = jnp.bfloat16


def _tile(seq):
    return min(512, seq)


def _dot(a, b):
    return jnp.dot(a, b, preferred_element_type=F32)


def _rms(x, g):
    ms = jnp.mean(x * x, axis=-1, keepdims=True)
    return x * lax.rsqrt(ms + EPS) * g


def _split3(c):
    hi = c.astype(BF16).astype(F32)
    r = c - hi
    mid = r.astype(BF16).astype(F32)
    lo = r - mid
    return hi, mid, lo


ROPE_TILE = 2048


def _rope_kernel(pos_ref, invf_ref, cos_ref, sin_ref):
    ang = invf_ref[...] * pos_ref[...].astype(F32)
    cos_ref[...] = jnp.cos(ang)
    sin_ref[...] = jnp.sin(ang)


def _rope_tables(positions, tm):
    b, s = positions.shape
    inv_freq = 1.0 / (ROPE_THETA ** (jnp.arange(0, ROT_DIM, 2, dtype=F32) / ROT_DIM))
    spec = pl.BlockSpec((None, ROT_HALF, tm), lambda i, j: (i, 0, j))
    return pl.pallas_call(
        _rope_kernel,
        out_shape=(jax.ShapeDtypeStruct((b, ROT_HALF, s), F32),) * 2,
        grid=(b, s // tm),
        in_specs=[pl.BlockSpec((None, 1, tm), lambda i, j: (i, 0, j)),
                  pl.BlockSpec((ROT_HALF, 1), lambda i, j: (0, 0))],
        out_specs=(spec, spec),
        name="rope_tables",
    )(positions.reshape(b, 1, s), inv_freq.reshape(ROT_HALF, 1))


COL_GROUPS = 2


def _proj_kernel(x_ref, g_ref, wt_ref, cos_ref, sin_ref, bias_ref,
                 qd_ref, kd_ref, vd_ref, qf_ref, kf_ref, vf_ref, carry_ref):
    tm = x_ref.shape[0]
    tg = tm // COL_GROUPS
    groups = [slice(c0, c0 + tg) for c0 in range(0, tm, tg)]

    @pl.when(pl.program_id(1) == 0)
    def _():
        carry_ref[...] = jnp.zeros_like(carry_ref)

    h_ts = [_rms(x_ref[cs, :], g_ref[...]).T.astype(BF16) for cs in groups]
    ones_blk = jnp.ones((BF16_ROWS, tg), F32)

    def seg(gi, r0, n):
        return _dot(wt_ref[r0:r0 + n, :], h_ts[gi])

    def rot(y, cs):
        cosv = cos_ref[:, cs]
        sinv = sin_ref[:, cs]
        t1 = y[0:ROT_HALF]
        t2 = y[ROT_HALF:ROT_DIM]
        return jnp.concatenate([t1 * cosv - t2 * sinv, t2 * cosv + t1 * sinv, y[ROT_DIM:]], axis=0)

    def rot_head(y, cs):
        return jnp.concatenate([rot(y[0:HEAD_DIM], cs), rot(y[HEAD_DIM:], cs)], axis=0)

    for gi, cs in enumerate(groups):
        y = seg(gi, 0, DIFF_WIDTH)
        for hd in range(N_HEADS_DIFF):
            qd_ref[hd, :, cs] = (rot_head(y[hd * LANES:(hd + 1) * LANES], cs) * Q_SCALE).astype(BF16)
    for gi, cs in enumerate(groups):
        y = seg(gi, DIFF_WIDTH, DIFF_WIDTH)
        for hd in range(N_HEADS_DIFF):
            kd_ref[hd, cs, :] = rot_head(y[hd * LANES:(hd + 1) * LANES], cs).T.astype(BF16)
    for gi, cs in enumerate(groups):
        y = seg(gi, 2 * DIFF_WIDTH, DIFF_WIDTH)
        for hd in range(N_HEADS_DIFF):
            vd_ref[hd, gi] = jnp.concatenate(
                [y[hd * LANES:(hd + 1) * LANES], ones_blk], axis=0).astype(BF16)

    row16 = lax.broadcasted_iota(jnp.int32, (BF16_ROWS, tg), 0)
    tri = (lax.broadcasted_iota(jnp.int32, (tg, tg), 0)
           <= lax.broadcasted_iota(jnp.int32, (tg, tg), 1)).astype(BF16)
    carry = carry_ref[...]
    cs_all = []
    for gi in range(COL_GROUPS):
        z = seg(gi, F_ROW, BF16_ROWS) + bias_ref[...]
        log_f = -(jnp.maximum(-z, 0.0) + jnp.log1p(jnp.exp(-jnp.abs(z))))
        log_f = jnp.where(row16 < N_HEADS_FOX, log_f, 0.0)
        packed = jnp.concatenate(_split3(log_f), axis=0).astype(BF16)
        c3 = _dot(packed, tri)
        c = (c3[0:BF16_ROWS] + c3[BF16_ROWS:2 * BF16_ROWS] + c3[2 * BF16_ROWS:]
             + jnp.tile(carry, (1, tg // LANES)))
        carry = jnp.broadcast_to(c[:, tg - 1:tg], carry_ref.shape)
        cs_all.append(c)
    carry_ref[...] = carry

    row8 = lax.broadcasted_iota(jnp.int32, (SUBLANES, tg), 0)
    zpad = jnp.zeros((LANES - HEAD_DIM - SUBLANES, tg), F32)
    for gi, cs in enumerate(groups):
        yq = seg(gi, 3 * DIFF_WIDTH, FOX_WIDTH)
        yk = seg(gi, 3 * DIFF_WIDTH + FOX_WIDTH, FOX_WIDTH)
        yv = seg(gi, 3 * DIFF_WIDTH + 2 * FOX_WIDTH, FOX_WIDTH)
        c = cs_all[gi]
        for hf in range(N_HEADS_FOX):
            c_hi, c_mid, c_lo = _split3(
                jnp.broadcast_to(c[hf:hf + 1, :] * LOG2E, (SUBLANES, tg)))
            aug_q = jnp.where(row8 == 0, c_hi, jnp.where(row8 == 1, c_mid, jnp.where(
                row8 == 2, c_lo, jnp.where(row8 < 6, 1.0, 0.0))))
            aug_k = jnp.where(row8 < 3, 1.0, jnp.where(row8 == 3, -c_hi, jnp.where(
                row8 == 4, -c_mid, jnp.where(row8 == 5, -c_lo, 0.0))))
            feat = slice(hf * HEAD_DIM, (hf + 1) * HEAD_DIM)
            qf_ref[hf, :, cs] = jnp.concatenate(
                [yq[feat] * Q_SCALE, aug_q, zpad], axis=0).astype(BF16)
            kf_ref[hf, cs, :] = jnp.concatenate([yk[feat], aug_k, zpad], axis=0).T.astype(BF16)
            vf_ref[hf, gi] = jnp.concatenate([yv[feat], ones_blk], axis=0).astype(BF16)


def _project(x, g, wt, cos, sin, bias, t):
    b, s, _ = x.shape
    tm = COL_GROUPS * t
    tok = lambda i, j: (i, j, 0)
    const = lambda i, j: (0, 0)
    rope = pl.BlockSpec((None, ROT_HALF, tm), lambda i, j: (i, 0, j))
    sd = jax.ShapeDtypeStruct

    def specs(n, v_rows):
        return (
            (sd((b, n, LANES, s), BF16), pl.BlockSpec((None, n, LANES, tm), lambda i, j: (i, 0, 0, j))),
            (sd((b, n, s, LANES), BF16), pl.BlockSpec((None, n, tm, LANES), lambda i, j: (i, 0, j, 0))),
            (sd((b, n, s // t, v_rows, t), BF16),
             pl.BlockSpec((None, n, COL_GROUPS, v_rows, t), lambda i, j: (i, 0, j, 0, 0))),
        )

    outs = specs(N_HEADS_DIFF, V_DIFF_ROWS) + specs(N_HEADS_FOX, V_FOX_ROWS)
    return pl.pallas_call(
        _proj_kernel,
        out_shape=tuple(o[0] for o in outs),
        grid=(b, s // tm),
        in_specs=[pl.BlockSpec((None, tm, D_MODEL), tok),
                  pl.BlockSpec((1, D_MODEL), const),
                  pl.BlockSpec((IN_ROWS, D_MODEL), const, pipeline_mode=pl.Buffered(1)),
                  rope, rope,
                  pl.BlockSpec((BF16_ROWS, 1), const)],
        out_specs=tuple(o[1] for o in outs),
        scratch_shapes=[pltpu.VMEM((BF16_ROWS, LANES), F32)],
        compiler_params=pltpu.CompilerParams(
            dimension_semantics=("arbitrary", "arbitrary"), vmem_limit_bytes=VMEM_LIMIT),
        name="in_proj",
    )(x, g, wt, cos, sin, bias)


DIAG_BUF = 2


def _attend(qi, n_blocks, q_cur, q_next, k_at, v_at, s_buf, mc_buf, m_sc, acc_sc):
    n = len(q_cur)
    t = m_sc.shape[-1]
    half = t // 2
    lo, hi = slice(0, half), slice(half, t)
    m_sc[...] = jnp.full_like(m_sc, NEG_INF)
    acc_sc[...] = jnp.zeros_like(acc_sc)

    def stage_a(i, buf, q_ts, blk):
        s_t = _dot(k_at(i, blk), q_ts[i]())
        s_buf[buf, i] = s_t
        mc_buf[buf, i] = jnp.max(s_t, axis=0, keepdims=True)

    def stage_bc(i, buf, blk):
        m_prev = m_sc[i]
        m_new = jnp.maximum(m_prev, mc_buf[buf, i])
        alpha = jnp.exp2(m_prev - m_new)
        p_t = jnp.exp2(s_buf[buf, i] - m_new).astype(BF16)
        acc_sc[i] = alpha * acc_sc[i] + _dot(v_at(i, blk), p_t)
        m_sc[i] = m_new

    def stage_a_diag(i, q_ts, blk):
        q_t = q_ts[i]()
        k = k_at(i, blk)
        keep = (lax.broadcasted_iota(jnp.int32, (half, half), 1)
                >= lax.broadcasted_iota(jnp.int32, (half, half), 0))
        s_ll = jnp.where(keep, _dot(k[lo], q_t[:, lo]), NEG_INF)
        s_lh = _dot(k[lo], q_t[:, hi])
        s_hh = jnp.where(keep, _dot(k[hi], q_t[:, hi]), NEG_INF)
        s_buf[DIAG_BUF, i, lo, lo] = s_ll
        s_buf[DIAG_BUF, i, lo, hi] = s_lh
        s_buf[DIAG_BUF, i, hi, hi] = s_hh
        mc_buf[DIAG_BUF, i, :, lo] = jnp.max(s_ll, axis=0, keepdims=True)
        mc_buf[DIAG_BUF, i, :, hi] = jnp.maximum(jnp.max(s_lh, axis=0, keepdims=True),
                                                 jnp.max(s_hh, axis=0, keepdims=True))

    def stage_bc_diag(i, blk):
        m_prev = m_sc[i]
        m_new = jnp.maximum(m_prev, mc_buf[DIAG_BUF, i])
        alpha = jnp.exp2(m_prev - m_new)
        v_t = v_at(i, blk)
        p_lo = jnp.exp2(s_buf[DIAG_BUF, i, lo, lo] - m_new[:, lo]).astype(BF16)
        p_hi = jnp.exp2(s_buf[DIAG_BUF, i, :, hi] - m_new[:, hi]).astype(BF16)
        acc_sc[i, :, lo] = alpha[:, lo] * acc_sc[i, :, lo] + _dot(v_t[:, lo], p_lo)
        acc_sc[i, :, hi] = alpha[:, hi] * acc_sc[i, :, hi] + _dot(v_t, p_hi)
        m_sc[i] = m_new

    def overlap(a=None, bc=None):
        for i in range(n):
            if a is not None:
                buf, q_ts, blk = a
                stage_a_diag(i, q_ts, blk) if buf == DIAG_BUF else stage_a(i, buf, q_ts, blk)
            if bc is not None:
                buf, blk = bc
                stage_bc_diag(i, blk) if buf == DIAG_BUF else stage_bc(i, buf, blk)

    next_diag = (DIAG_BUF, q_next, jnp.minimum(qi + 1, n_blocks - 1))

    @pl.when(qi == 0)
    def _():
        overlap(a=(DIAG_BUF, q_cur, qi))
        overlap(bc=(DIAG_BUF, qi))
        overlap(a=next_diag)

    @pl.when(qi == 1)
    def _():
        overlap(a=(1, q_cur, 0), bc=(DIAG_BUF, qi))
        overlap(a=next_diag, bc=(1, 0))

    @pl.when(qi >= 2)
    def _():
        overlap(a=(1, q_cur, 0), bc=(DIAG_BUF, qi))
        overlap(a=(0, q_cur, 1), bc=(1, 0))

        def pair(ii, carry):
            i = 2 * ii
            overlap(a=(1, q_cur, i), bc=(0, i - 1))
            overlap(a=(0, q_cur, i + 1), bc=(1, i))
            return carry

        lax.fori_loop(1, qi // 2, pair, 0)

        @pl.when(qi % 2 == 1)
        def _():
            overlap(a=(1, q_cur, qi - 1), bc=(0, qi - 2))
            overlap(a=next_diag, bc=(1, qi - 1))

        @pl.when(qi % 2 == 0)
        def _():
            overlap(a=next_diag, bc=(0, qi - 1))


MAPS_PER_STEP = 4
DIFF_GROUP = MAPS_PER_STEP // 2
FOX_GROUP = MAPS_PER_STEP


def _next_q_block(n_blocks):
    return lambda i, h, j: (i, h, 0, jnp.minimum(j + 1, n_blocks - 1))


def _attend_scratch(v_rows, t):
    n = MAPS_PER_STEP
    return [pltpu.VMEM((DIAG_BUF + 1, n, t, t), F32), pltpu.VMEM((DIAG_BUF + 1, n, 1, t), F32),
            pltpu.VMEM((n, 1, t), F32), pltpu.VMEM((n, v_rows, t), F32)]


def _diff_attn_kernel(lam_ref, g_ref, q_ref, qn_ref, k_ref, v_ref, o_ref,
                      s_buf, mc_buf, m_sc, acc_sc, *, lam_init):
    t = q_ref.shape[2]
    qi = pl.program_id(2)
    zero = jnp.zeros((HEAD_DIM, t), BF16)

    def maps_of(ref):
        q_ts = []
        for hd in range(DIFF_GROUP):
            q_ts.append(lambda hd=hd: jnp.concatenate([ref[hd, 0:HEAD_DIM, :], zero], axis=0))
            q_ts.append(lambda hd=hd: jnp.concatenate([zero, ref[hd, HEAD_DIM:, :]], axis=0))
        return q_ts

    k_at = lambda i, j: k_ref[i // 2, pl.ds(pl.multiple_of(j * t, t), t), :]
    v_at = lambda i, j: v_ref[i // 2, j]
    _attend(qi, v_ref.shape[1], maps_of(q_ref), maps_of(qn_ref), k_at, v_at,
            s_buf, mc_buf, m_sc, acc_sc)

    lam = (jnp.exp(jnp.sum(lam_ref[0:1, :] * lam_ref[1:2, :], axis=1, keepdims=True))
           - jnp.exp(jnp.sum(lam_ref[2:3, :] * lam_ref[3:4, :], axis=1, keepdims=True)) + lam_init)
    nv = 2 * HEAD_DIM
    outs = []
    for hd in range(DIFF_GROUP):
        a0 = acc_sc[2 * hd]
        a1 = acc_sc[2 * hd + 1]
        o_t = a0[0:nv] / a0[nv:nv + 1] - lam * (a1[0:nv] / a1[nv:nv + 1])
        ms = jnp.mean(o_t * o_t, axis=0, keepdims=True)
        outs.append(o_t * lax.rsqrt(ms + EPS) * g_ref[...] * (1.0 - lam_init))
    o_ref[...] = jnp.concatenate(outs, axis=0).T.astype(o_ref.dtype)


def _diff_attention(lam_vecs, g_col, qd, kd, vd, t, lam_init):
    b, nh, s, _ = kd.shape
    nkb = s // t
    g = DIFF_GROUP
    return pl.pallas_call(
        functools.partial(_diff_attn_kernel, lam_init=lam_init),
        out_shape=jax.ShapeDtypeStruct((b, s, DIFF_WIDTH), BF16),
        grid=(b, nh // g, nkb),
        in_specs=[pl.BlockSpec((SUBLANES, LANES), lambda i, h, j: (0, 0)),
                  pl.BlockSpec((LANES, 1), lambda i, h, j: (0, 0)),
                  pl.BlockSpec((None, g, LANES, t), lambda i, h, j: (i, h, 0, j)),
                  pl.BlockSpec((None, g, LANES, t), _next_q_block(nkb)),
                  pl.BlockSpec((None, g, s, LANES), lambda i, h, j: (i, h, 0, 0)),
                  pl.BlockSpec((None, g, nkb, V_DIFF_ROWS, t), lambda i, h, j: (i, h, 0, 0, 0))],
        out_specs=pl.BlockSpec((None, t, g * 2 * HEAD_DIM), lambda i, h, j: (i, j, h)),
        scratch_shapes=_attend_scratch(V_DIFF_ROWS, t),
        compiler_params=pltpu.CompilerParams(
            dimension_semantics=("arbitrary",) * 3, vmem_limit_bytes=VMEM_LIMIT),
        name="diff_attention",
    )(lam_vecs, g_col, qd, qd, kd, vd)


def _fox_attn_kernel(q_ref, qn_ref, k_ref, v_ref, o_ref, s_buf, mc_buf, m_sc, acc_sc):
    t = q_ref.shape[2]
    qi = pl.program_id(2)
    maps_of = lambda ref: tuple((lambda hh=hh: ref[hh]) for hh in range(FOX_GROUP))
    k_at = lambda i, j: k_ref[i, pl.ds(pl.multiple_of(j * t, t), t), :]
    v_at = lambda i, j: v_ref[i, j]
    _attend(qi, v_ref.shape[1], maps_of(q_ref), maps_of(qn_ref), k_at, v_at,
            s_buf, mc_buf, m_sc, acc_sc)

    outs = []
    for hh in range(FOX_GROUP):
        a = acc_sc[hh]
        outs.append(a[0:HEAD_DIM] / a[HEAD_DIM:HEAD_DIM + 1])
    o_ref[...] = jnp.concatenate(outs, axis=0).T.astype(o_ref.dtype)


def _fox_attention(qf, kf, vf, t):
    b, nh, s, _ = kf.shape
    nkb = s // t
    g = FOX_GROUP
    return pl.pallas_call(
        _fox_attn_kernel,
        out_shape=jax.ShapeDtypeStruct((b, s, FOX_WIDTH), BF16),
        grid=(b, nh // g, nkb),
        in_specs=[pl.BlockSpec((None, g, LANES, t), lambda i, h, j: (i, h, 0, j)),
                  pl.BlockSpec((None, g, LANES, t), _next_q_block(nkb)),
                  pl.BlockSpec((None, g, s, LANES), lambda i, h, j: (i, h, 0, 0)),
                  pl.BlockSpec((None, g, nkb, V_FOX_ROWS, t), lambda i, h, j: (i, h, 0, 0, 0))],
        out_specs=pl.BlockSpec((None, t, g * HEAD_DIM), lambda i, h, j: (i, j, h)),
        scratch_shapes=_attend_scratch(V_FOX_ROWS, t),
        compiler_params=pltpu.CompilerParams(
            dimension_semantics=("arbitrary",) * 3, vmem_limit_bytes=VMEM_LIMIT),
        name="fox_attention",
    )(qf, qf, kf, vf)


MXU_TILE = 256
FF_CHUNKS = ((0, 6 * MXU_TILE), (6 * MXU_TILE, D_FF))
ROW_SPLIT = 2


def _mix_ffn_kernel(oa_ref, ob_ref, x_ref, wa_ref, wb_ref, gmix_ref, gpre_ref,
                    wg_ref, wu_ref, wd_ref, gpost_ref, o_ref):
    tm = x_ref.shape[0]
    halves = [slice(r0, r0 + tm // ROW_SPLIT) for r0 in range(0, tm, tm // ROW_SPLIT)]
    xs, hs = [], []
    for rs in halves:
        y = _dot(oa_ref[rs, :], wa_ref[...]) + _dot(ob_ref[rs, :], wb_ref[...])
        x = x_ref[rs, :] + _rms(y, gmix_ref[...])
        xs.append(x)
        hs.append(_rms(x, gpre_ref[...]).astype(BF16))
    ys = [None] * len(halves)
    for c0, c1 in FF_CHUNKS:
        for k in range(len(halves)):
            gate = _dot(hs[k], wg_ref[:, c0:c1])
            up = _dot(hs[k], wu_ref[:, c0:c1])
            act = (gate * jax.nn.sigmoid(gate) * up).astype(BF16)
            part = _dot(act, wd_ref[c0:c1, :])
            ys[k] = part if ys[k] is None else ys[k] + part
    for k, rs in enumerate(halves):
        o_ref[rs, :] = xs[k] + _rms(ys[k], gpost_ref[...])


def _mix_ffn(oa, ob, x, wa, wb, gmix, gpre, wg, wu, wd, gpost, tm):
    b, s, _ = x.shape
    tok = lambda i, j: (i, j, 0)
    const = lambda i, j: (0, 0)
    resident = lambda shape: pl.BlockSpec(shape, const, pipeline_mode=pl.Buffered(1))
    gain = pl.BlockSpec((1, D_MODEL), const)
    return pl.pallas_call(
        _mix_ffn_kernel,
        out_shape=jax.ShapeDtypeStruct(x.shape, F32),
        grid=(b, s // tm),
        in_specs=[pl.BlockSpec((None, tm, DIFF_WIDTH), tok),
                  pl.BlockSpec((None, tm, FOX_WIDTH), tok),
                  pl.BlockSpec((None, tm, D_MODEL), tok),
                  resident((DIFF_WIDTH, D_MODEL)), resident((FOX_WIDTH, D_MODEL)), gain, gain,
                  resident((D_MODEL, D_FF)), resident((D_MODEL, D_FF)), resident((D_FF, D_MODEL)),
                  gain],
        out_specs=pl.BlockSpec((None, tm, D_MODEL), tok),
        compiler_params=pltpu.CompilerParams(
            dimension_semantics=("arbitrary", "arbitrary"), vmem_limit_bytes=VMEM_LIMIT),
        name="mix_ffn",
    )(oa, ob, x, wa, wb, gmix, gpre, wg, wu, wd, gpost)


def kernel(x, positions, attn_pre_g, w_in, forget_bias, lam_q1, lam_k1, lam_q2, lam_k2,
           diff_sub_g, w_out, attn_post_g, ffn_pre_g, w_gate, w_up, w_down, ffn_post_g):
    depth = w_in.shape[0]
    seq = x.shape[1]
    t = _tile(seq)
    assert seq % (COL_GROUPS * t) == 0 and t % LANES == 0 and x.shape[2] == D_MODEL

    cos, sin = _rope_tables(positions, ROPE_TILE if seq % ROPE_TILE == 0 else t)
    w_in_t = jnp.pad(jnp.swapaxes(w_in, 1, 2),
                     ((0, 0), (0, IN_ROWS - w_in.shape[2]), (0, 0))).astype(BF16)
    w_out_b = w_out.astype(BF16)
    w_gate_b = w_gate.astype(BF16)
    w_up_b = w_up.astype(BF16)
    w_down_b = w_down.astype(BF16)
    bias = jnp.pad(forget_bias.astype(F32), ((0, 0), (0, BF16_ROWS - N_HEADS_FOX)))[:, :, None]
    lam_vecs = jnp.pad(jnp.stack([lam_q1, lam_k1, lam_q2, lam_k2], axis=1).astype(F32),
                       ((0, 0), (0, SUBLANES - 4), (0, LANES - HEAD_DIM)))
    row = lambda v: v.reshape(1, -1).astype(F32)

    for l in range(depth):
        lam_init = 0.8 - 0.6 * math.exp(-0.3 * l)
        qd, kd, vd, qf, kf, vf = _project(x, row(attn_pre_g[l]), w_in_t[l], cos, sin, bias[l], t)
        oa = _diff_attention(lam_vecs[l], diff_sub_g[l].reshape(-1, 1).astype(F32), qd, kd, vd,
                             t, lam_init)
        ob = _fox_attention(qf, kf, vf, t)
        x = _mix_ffn(oa, ob, x, w_out_b[l, :DIFF_WIDTH], w_out_b[l, DIFF_WIDTH:],
                     row(attn_post_g[l]), row(ffn_pre_g[l]), w_gate_b[l], w_up_b[l], w_down_b[l],
                     row(ffn_post_g[l]), t)
    return x
```

```python
import functools
import math

import jax
import jax.numpy as jnp
from jax import lax
from jax.experimental import pallas as pl
from jax.experimental.pallas import tpu as pltpu

D_MODEL = 1024
HEAD_DIM = 64
N_HEADS_DIFF = 4
N_HEADS_FOX = 8
DIFF_WIDTH = N_HEADS_DIFF * 2 * HEAD_DIM
FOX_WIDTH = N_HEADS_FOX * HEAD_DIM
ROT_DIM = HEAD_DIM // 4
ROT_HALF = ROT_DIM // 2
ROPE_THETA = 500000.0
D_FF = 2816
EPS = 1e-6
NEG_INF = -1e30
LOG2E = 1.4426950408889634
Q_SCALE = HEAD_DIM ** -0.5 * LOG2E

LANES = 128
SUBLANES = 8
BF16_ROWS = 16
F_ROW = 3 * DIFF_WIDTH + 3 * FOX_WIDTH
IN_ROWS = F_ROW + BF16_ROWS
V_DIFF_ROWS = 2 * HEAD_DIM + BF16_ROWS
V_FOX_ROWS = HEAD_DIM + BF16_ROWS
VMEM_LIMIT = 56 * 1024 * 1024

F32 = jnp.float32
BF16 = jnp.bfloat16


def _tile(seq):
    return min(512, seq)


def _dot(a, b):
    return jnp.dot(a, b, preferred_element_type=F32)


def _rms(x, g):
    ms = jnp.mean(x * x, axis=-1, keepdims=True)
    return x * lax.rsqrt(ms + EPS) * g


def _split3(c):
    hi = c.astype(BF16).astype(F32)
    r = c - hi
    mid = r.astype(BF16).astype(F32)
    lo = r - mid
    return hi, mid, lo


ROPE_TILE = 2048


def _rope_kernel(pos_ref, invf_ref, cos_ref, sin_ref):
    ang = invf_ref[...] * pos_ref[...].astype(F32)
    cos_ref[...] = jnp.cos(ang)
    sin_ref[...] = jnp.sin(ang)


def _rope_tables(positions, tm):
    b, s = positions.shape
    inv_freq = 1.0 / (ROPE_THETA ** (jnp.arange(0, ROT_DIM, 2, dtype=F32) / ROT_DIM))
    spec = pl.BlockSpec((None, ROT_HALF, tm), lambda i, j: (i, 0, j))
    return pl.pallas_call(
        _rope_kernel,
        out_shape=(jax.ShapeDtypeStruct((b, ROT_HALF, s), F32),) * 2,
        grid=(b, s // tm),
        in_specs=[pl.BlockSpec((None, 1, tm), lambda i, j: (i, 0, j)),
                  pl.BlockSpec((ROT_HALF, 1), lambda i, j: (0, 0))],
        out_specs=(spec, spec),
        name="rope_tables",
    )(positions.reshape(b, 1, s), inv_freq.reshape(ROT_HALF, 1))


COL_GROUPS = 2


def _proj_kernel(x_ref, g_ref, wt_ref, cos_ref, sin_ref, bias_ref,
                 qd_ref, kd_ref, vd_ref, qf_ref, kf_ref, vf_ref, carry_ref):
    tm = x_ref.shape[0]
    tg = tm // COL_GROUPS
    groups = [slice(c0, c0 + tg) for c0 in range(0, tm, tg)]

    @pl.when(pl.program_id(1) == 0)
    def _():
        carry_ref[...] = jnp.zeros_like(carry_ref)

    h_ts = [_rms(x_ref[cs, :], g_ref[...]).T.astype(BF16) for cs in groups]
    ones_blk = jnp.ones((BF16_ROWS, tg), F32)

    def seg(gi, r0, n):
        return _dot(wt_ref[r0:r0 + n, :], h_ts[gi])

    def rot(y, cs):
        cosv = cos_ref[:, cs]
        sinv = sin_ref[:, cs]
        t1 = y[0:ROT_HALF]
        t2 = y[ROT_HALF:ROT_DIM]
        return jnp.concatenate([t1 * cosv - t2 * sinv, t2 * cosv + t1 * sinv, y[ROT_DIM:]], axis=0)

    def rot_head(y, cs):
        return jnp.concatenate([rot(y[0:HEAD_DIM], cs), rot(y[HEAD_DIM:], cs)], axis=0)

    for gi, cs in enumerate(groups):
        y = seg(gi, 0, DIFF_WIDTH)
        for hd in range(N_HEADS_DIFF):
            qd_ref[hd, :, cs] = (rot_head(y[hd * LANES:(hd + 1) * LANES], cs) * Q_SCALE).astype(BF16)
    for gi, cs in enumerate(groups):
        y = seg(gi, DIFF_WIDTH, DIFF_WIDTH)
        for hd in range(N_HEADS_DIFF):
            kd_ref[hd, cs, :] = rot_head(y[hd * LANES:(hd + 1) * LANES], cs).T.astype(BF16)
    for gi, cs in enumerate(groups):
        y = seg(gi, 2 * DIFF_WIDTH, DIFF_WIDTH)
        for hd in range(N_HEADS_DIFF):
            vd_ref[hd, gi] = jnp.concatenate(
                [y[hd * LANES:(hd + 1) * LANES], ones_blk], axis=0).astype(BF16)

    row16 = lax.broadcasted_iota(jnp.int32, (BF16_ROWS, tg), 0)
    tri = (lax.broadcasted_iota(jnp.int32, (tg, tg), 0)
           <= lax.broadcasted_iota(jnp.int32, (tg, tg), 1)).astype(BF16)
    carry = carry_ref[...]
    cs_all = []
    for gi in range(COL_GROUPS):
        z = seg(gi, F_ROW, BF16_ROWS) + bias_ref[...]
        log_f = -(jnp.maximum(-z, 0.0) + jnp.log1p(jnp.exp(-jnp.abs(z))))
        log_f = jnp.where(row16 < N_HEADS_FOX, log_f, 0.0)
        packed = jnp.concatenate(_split3(log_f), axis=0).astype(BF16)
        c3 = _dot(packed, tri)
        c = (c3[0:BF16_ROWS] + c3[BF16_ROWS:2 * BF16_ROWS] + c3[2 * BF16_ROWS:]
             + jnp.tile(carry, (1, tg // LANES)))
        carry = jnp.broadcast_to(c[:, tg - 1:tg], carry_ref.shape)
        cs_all.append(c)
    carry_ref[...] = carry

    row8 = lax.broadcasted_iota(jnp.int32, (SUBLANES, tg), 0)
    zpad = jnp.zeros((LANES - HEAD_DIM - SUBLANES, tg), F32)
    for gi, cs in enumerate(groups):
        yq = seg(gi, 3 * DIFF_WIDTH, FOX_WIDTH)
        yk = seg(gi, 3 * DIFF_WIDTH + FOX_WIDTH, FOX_WIDTH)
        yv = seg(gi, 3 * DIFF_WIDTH + 2 * FOX_WIDTH, FOX_WIDTH)
        c = cs_all[gi]
        for hf in range(N_HEADS_FOX):
            c_hi, c_mid, c_lo = _split3(
                jnp.broadcast_to(c[hf:hf + 1, :] * LOG2E, (SUBLANES, tg)))
            aug_q = jnp.where(row8 == 0, c_hi, jnp.where(row8 == 1, c_mid, jnp.where(
                row8 == 2, c_lo, jnp.where(row8 < 6, 1.0, 0.0))))
            aug_k = jnp.where(row8 < 3, 1.0, jnp.where(row8 == 3, -c_hi, jnp.where(
                row8 == 4, -c_mid, jnp.where(row8 == 5, -c_lo, 0.0))))
            feat = slice(hf * HEAD_DIM, (hf + 1) * HEAD_DIM)
            qf_ref[hf, :, cs] = jnp.concatenate(
                [yq[feat] * Q_SCALE, aug_q, zpad], axis=0).astype(BF16)
            kf_ref[hf, cs, :] = jnp.concatenate([yk[feat], aug_k, zpad], axis=0).T.astype(BF16)
            vf_ref[hf, gi] = jnp.concatenate([yv[feat], ones_blk], axis=0).astype(BF16)


def _project(x, g, wt, cos, sin, bias, t):
    b, s, _ = x.shape
    tm = COL_GROUPS * t
    tok = lambda i, j: (i, j, 0)
    const = lambda i, j: (0, 0)
    rope = pl.BlockSpec((None, ROT_HALF, tm), lambda i, j: (i, 0, j))
    sd = jax.ShapeDtypeStruct

    def specs(n, v_rows):
        return (
            (sd((b, n, LANES, s), BF16), pl.BlockSpec((None, n, LANES, tm), lambda i, j: (i, 0, 0, j))),
            (sd((b, n, s, LANES), BF16), pl.BlockSpec((None, n, tm, LANES), lambda i, j: (i, 0, j, 0))),
            (sd((b, n, s // t, v_rows, t), BF16),
             pl.BlockSpec((None, n, COL_GROUPS, v_rows, t), lambda i, j: (i, 0, j, 0, 0))),
        )

    outs = specs(N_HEADS_DIFF, V_DIFF_ROWS) + specs(N_HEADS_FOX, V_FOX_ROWS)
    return pl.pallas_call(
        _proj_kernel,
        out_shape=tuple(o[0] for o in outs),
        grid=(b, s // tm),
        in_specs=[pl.BlockSpec((None, tm, D_MODEL), tok),
                  pl.BlockSpec((1, D_MODEL), const),
                  pl.BlockSpec((IN_ROWS, D_MODEL), const, pipeline_mode=pl.Buffered(1)),
                  rope, rope,
                  pl.BlockSpec((BF16_ROWS, 1), const)],
        out_specs=tuple(o[1] for o in outs),
        scratch_shapes=[pltpu.VMEM((BF16_ROWS, LANES), F32)],
        compiler_params=pltpu.CompilerParams(
            dimension_semantics=("arbitrary", "arbitrary"), vmem_limit_bytes=VMEM_LIMIT),
        name="in_proj",
    )(x, g, wt, cos, sin, bias)


DIAG_BUF = 2


def _attend(qi, n_blocks, q_cur, q_next, k_at, v_at, s_buf, mc_buf, m_sc, acc_sc):
    n = len(q_cur)
    t = m_sc.shape[-1]
    half = t // 2
    lo, hi = slice(0, half), slice(half, t)
    m_sc[...] = jnp.full_like(m_sc, NEG_INF)
    acc_sc[...] = jnp.zeros_like(acc_sc)

    def stage_a(i, buf, q_ts, blk):
        s_t = _dot(k_at(i, blk), q_ts[i]())
        s_buf[buf, i] = s_t
        mc_buf[buf, i] = jnp.max(s_t, axis=0, keepdims=True)

    def stage_bc(i, buf, blk):
        m_prev = m_sc[i]
        m_new = jnp.maximum(m_prev, mc_buf[buf, i])
        alpha = jnp.exp2(m_prev - m_new)
        p_t = jnp.exp2(s_buf[buf, i] - m_new).astype(BF16)
        acc_sc[i] = alpha * acc_sc[i] + _dot(v_at(i, blk), p_t)
        m_sc[i] = m_new

    def stage_a_diag(i, q_ts, blk):
        q_t = q_ts[i]()
        k = k_at(i, blk)
        keep = (lax.broadcasted_iota(jnp.int32, (half, half), 1)
                >= lax.broadcasted_iota(jnp.int32, (half, half), 0))
        s_ll = jnp.where(keep, _dot(k[lo], q_t[:, lo]), NEG_INF)
        s_lh = _dot(k[lo], q_t[:, hi])
        s_hh = jnp.where(keep, _dot(k[hi], q_t[:, hi]), NEG_INF)
        s_buf[DIAG_BUF, i, lo, lo] = s_ll
        s_buf[DIAG_BUF, i, lo, hi] = s_lh
        s_buf[DIAG_BUF, i, hi, hi] = s_hh
        mc_buf[DIAG_BUF, i, :, lo] = jnp.max(s_ll, axis=0, keepdims=True)
        mc_buf[DIAG_BUF, i, :, hi] = jnp.maximum(jnp.max(s_lh, axis=0, keepdims=True),
                                                 jnp.max(s_hh, axis=0, keepdims=True))

    def stage_bc_diag(i, blk):
        m_prev = m_sc[i]
        m_new = jnp.maximum(m_prev, mc_buf[DIAG_BUF, i])
        alpha = jnp.exp2(m_prev - m_new)
        v_t = v_at(i, blk)
        p_lo = jnp.exp2(s_buf[DIAG_BUF, i, lo, lo] - m_new[:, lo]).astype(BF16)
        p_hi = jnp.exp2(s_buf[DIAG_BUF, i, :, hi] - m_new[:, hi]).astype(BF16)
        acc_sc[i, :, lo] = alpha[:, lo] * acc_sc[i, :, lo] + _dot(v_t[:, lo], p_lo)
        acc_sc[i, :, hi] = alpha[:, hi] * acc_sc[i, :, hi] + _dot(v_t, p_hi)
        m_sc[i] = m_new

    def overlap(a=None, bc=None):
        for i in range(n):
            if a is not None:
                buf, q_ts, blk = a
                stage_a_diag(i, q_ts, blk) if buf == DIAG_BUF else stage_a(i, buf, q_ts, blk)
            if bc is not None:
                buf, blk = bc
                stage_bc_diag(i, blk) if buf == DIAG_BUF else stage_bc(i, buf, blk)

    next_diag = (DIAG_BUF, q_next, jnp.minimum(qi + 1, n_blocks - 1))

    @pl.when(qi == 0)
    def _():
        overlap(a=(DIAG_BUF, q_cur, qi))
        overlap(bc=(DIAG_BUF, qi))
        overlap(a=next_diag)

    @pl.when(qi == 1)
    def _():
        overlap(a=(1, q_cur, 0), bc=(DIAG_BUF, qi))
        overlap(a=next_diag, bc=(1, 0))

    @pl.when(qi >= 2)
    def _():
        overlap(a=(1, q_cur, 0), bc=(DIAG_BUF, qi))
        overlap(a=(0, q_cur, 1), bc=(1, 0))

        def pair(ii):
            i = 2 * ii
            overlap(a=(1, q_cur, i), bc=(0, i - 1))
            overlap(a=(0, q_cur, i + 1), bc=(1, i))

        n_pairs = qi // 2 - 1

        def two_pairs(jj, carry):
            pair(1 + 2 * jj)
            pair(2 + 2 * jj)
            return carry

        lax.fori_loop(0, n_pairs // 2, two_pairs, 0)

        @pl.when(n_pairs % 2 == 1)
        def _():
            pair(n_pairs)

        @pl.when(qi % 2 == 1)
        def _():
            overlap(a=(1, q_cur, qi - 1), bc=(0, qi - 2))
            overlap(a=next_diag, bc=(1, qi - 1))

        @pl.when(qi % 2 == 0)
        def _():
            overlap(a=next_diag, bc=(0, qi - 1))


MAPS_PER_STEP = 4
DIFF_GROUP = MAPS_PER_STEP // 2
FOX_GROUP = MAPS_PER_STEP


def _next_q_block(n_blocks):
    return lambda i, h, j: (i, h, 0, jnp.minimum(j + 1, n_blocks - 1))


def _attend_scratch(v_rows, t):
    n = MAPS_PER_STEP
    return [pltpu.VMEM((DIAG_BUF + 1, n, t, t), F32), pltpu.VMEM((DIAG_BUF + 1, n, 1, t), F32),
            pltpu.VMEM((n, 1, t), F32), pltpu.VMEM((n, v_rows, t), F32)]


def _diff_attn_kernel(lam_ref, g_ref, q_ref, qn_ref, k_ref, v_ref, o_ref,
                      s_buf, mc_buf, m_sc, acc_sc, *, lam_init):
    t = q_ref.shape[2]
    qi = pl.program_id(2)
    zero = jnp.zeros((HEAD_DIM, t), BF16)

    def maps_of(ref):
        q_ts = []
        for hd in range(DIFF_GROUP):
            q_ts.append(lambda hd=hd: jnp.concatenate([ref[hd, 0:HEAD_DIM, :], zero], axis=0))
            q_ts.append(lambda hd=hd: jnp.concatenate([zero, ref[hd, HEAD_DIM:, :]], axis=0))
        return q_ts

    k_at = lambda i, j: k_ref[i // 2, pl.ds(pl.multiple_of(j * t, t), t), :]
    v_at = lambda i, j: v_ref[i // 2, j]
    _attend(qi, v_ref.shape[1], maps_of(q_ref), maps_of(qn_ref), k_at, v_at,
            s_buf, mc_buf, m_sc, acc_sc)

    lam = (jnp.exp(jnp.sum(lam_ref[0:1, :] * lam_ref[1:2, :], axis=1, keepdims=True))
           - jnp.exp(jnp.sum(lam_ref[2:3, :] * lam_ref[3:4, :], axis=1, keepdims=True)) + lam_init)
    nv = 2 * HEAD_DIM
    outs = []
    for hd in range(DIFF_GROUP):
        a0 = acc_sc[2 * hd]
        a1 = acc_sc[2 * hd + 1]
        o_t = a0[0:nv] / a0[nv:nv + 1] - lam * (a1[0:nv] / a1[nv:nv + 1])
        ms = jnp.mean(o_t * o_t, axis=0, keepdims=True)
        outs.append(o_t * lax.rsqrt(ms + EPS) * g_ref[...] * (1.0 - lam_init))
    o_ref[...] = jnp.concatenate(outs, axis=0).T.astype(o_ref.dtype)


def _diff_attention(lam_vecs, g_col, qd, kd, vd, t, lam_init):
    b, nh, s, _ = kd.shape
    nkb = s // t
    g = DIFF_GROUP
    return pl.pallas_call(
        functools.partial(_diff_attn_kernel, lam_init=lam_init),
        out_shape=jax.ShapeDtypeStruct((b, s, DIFF_WIDTH), BF16),
        grid=(b, nh // g, nkb),
        in_specs=[pl.BlockSpec((SUBLANES, LANES), lambda i, h, j: (0, 0)),
                  pl.BlockSpec((LANES, 1), lambda i, h, j: (0, 0)),
                  pl.BlockSpec((None, g, LANES, t), lambda i, h, j: (i, h, 0, j)),
                  pl.BlockSpec((None, g, LANES, t), _next_q_block(nkb)),
                  pl.BlockSpec((None, g, s, LANES), lambda i, h, j: (i, h, 0, 0)),
                  pl.BlockSpec((None, g, nkb, V_DIFF_ROWS, t), lambda i, h, j: (i, h, 0, 0, 0))],
        out_specs=pl.BlockSpec((None, t, g * 2 * HEAD_DIM), lambda i, h, j: (i, j, h)),
        scratch_shapes=_attend_scratch(V_DIFF_ROWS, t),
        compiler_params=pltpu.CompilerParams(
            dimension_semantics=("arbitrary",) * 3, vmem_limit_bytes=VMEM_LIMIT),
        name="diff_attention",
    )(lam_vecs, g_col, qd, qd, kd, vd)


def _fox_attn_kernel(q_ref, qn_ref, k_ref, v_ref, o_ref, s_buf, mc_buf, m_sc, acc_sc):
    t = q_ref.shape[2]
    qi = pl.program_id(2)
    maps_of = lambda ref: tuple((lambda hh=hh: ref[hh]) for hh in range(FOX_GROUP))
    k_at = lambda i, j: k_ref[i, pl.ds(pl.multiple_of(j * t, t), t), :]
    v_at = lambda i, j: v_ref[i, j]
    _attend(qi, v_ref.shape[1], maps_of(q_ref), maps_of(qn_ref), k_at, v_at,
            s_buf, mc_buf, m_sc, acc_sc)

    outs = []
    for hh in range(FOX_GROUP):
        a = acc_sc[hh]
        outs.append(a[0:HEAD_DIM] / a[HEAD_DIM:HEAD_DIM + 1])
    o_ref[...] = jnp.concatenate(outs, axis=0).T.astype(o_ref.dtype)


def _fox_attention(qf, kf, vf, t):
    b, nh, s, _ = kf.shape
    nkb = s // t
    g = FOX_GROUP
    return pl.pallas_call(
        _fox_attn_kernel,
        out_shape=jax.ShapeDtypeStruct((b, s, FOX_WIDTH), BF16),
        grid=(b, nh // g, nkb),
        in_specs=[pl.BlockSpec((None, g, LANES, t), lambda i, h, j: (i, h, 0, j)),
                  pl.BlockSpec((None, g, LANES, t), _next_q_block(nkb)),
                  pl.BlockSpec((None, g, s, LANES), lambda i, h, j: (i, h, 0, 0)),
                  pl.BlockSpec((None, g, nkb, V_FOX_ROWS, t), lambda i, h, j: (i, h, 0, 0, 0))],
        out_specs=pl.BlockSpec((None, t, g * HEAD_DIM), lambda i, h, j: (i, j, h)),
        scratch_shapes=_attend_scratch(V_FOX_ROWS, t),
        compiler_params=pltpu.CompilerParams(
            dimension_semantics=("arbitrary",) * 3, vmem_limit_bytes=VMEM_LIMIT),
        name="fox_attention",
    )(qf, qf, kf, vf)


MXU_TILE = 256
FF_CHUNKS = ((0, 6 * MXU_TILE), (6 * MXU_TILE, D_FF))
ROW_SPLIT = 2


def _mix_ffn_kernel(oa_ref, ob_ref, x_ref, wa_ref, wb_ref, gmix_ref, gpre_ref,
                    wg_ref, wu_ref, wd_ref, gpost_ref, o_ref):
    tm = x_ref.shape[0]
    halves = [slice(r0, r0 + tm // ROW_SPLIT) for r0 in range(0, tm, tm // ROW_SPLIT)]
    xs, hs = [], []
    for rs in halves:
        y = _dot(oa_ref[rs, :], wa_ref[...]) + _dot(ob_ref[rs, :], wb_ref[...])
        x = x_ref[rs, :] + _rms(y, gmix_ref[...])
        xs.append(x)
        hs.append(_rms(x, gpre_ref[...]).astype(BF16))
    ys = [None] * len(halves)
    for c0, c1 in FF_CHUNKS:
        for k in range(len(halves)):
            gate = _dot(hs[k], wg_ref[:, c0:c1])
            up = _dot(hs[k], wu_ref[:, c0:c1])
            act = (gate * jax.nn.sigmoid(gate) * up).astype(BF16)
            part = _dot(act, wd_ref[c0:c1, :])
            ys[k] = part if ys[k] is None else ys[k] + part
    for k, rs in enumerate(halves):
        o_ref[rs, :] = xs[k] + _rms(ys[k], gpost_ref[...])


def _mix_ffn(oa, ob, x, wa, wb, gmix, gpre, wg, wu, wd, gpost, tm):
    b, s, _ = x.shape
    tok = lambda i, j: (i, j, 0)
    const = lambda i, j: (0, 0)
    resident = lambda shape: pl.BlockSpec(shape, const, pipeline_mode=pl.Buffered(1))
    gain = pl.BlockSpec((1, D_MODEL), const)
    return pl.pallas_call(
        _mix_ffn_kernel,
        out_shape=jax.ShapeDtypeStruct(x.shape, F32),
        grid=(b, s // tm),
        in_specs=[pl.BlockSpec((None, tm, DIFF_WIDTH), tok),
                  pl.BlockSpec((None, tm, FOX_WIDTH), tok),
                  pl.BlockSpec((None, tm, D_MODEL), tok),
                  resident((DIFF_WIDTH, D_MODEL)), resident((FOX_WIDTH, D_MODEL)), gain, gain,
                  resident((D_MODEL, D_FF)), resident((D_MODEL, D_FF)), resident((D_FF, D_MODEL)),
                  gain],
        out_specs=pl.BlockSpec((None, tm, D_MODEL), tok),
        compiler_params=pltpu.CompilerParams(
            dimension_semantics=("arbitrary", "arbitrary"), vmem_limit_bytes=VMEM_LIMIT),
        name="mix_ffn",
    )(oa, ob, x, wa, wb, gmix, gpre, wg, wu, wd, gpost)


def kernel(x, positions, attn_pre_g, w_in, forget_bias, lam_q1, lam_k1, lam_q2, lam_k2,
           diff_sub_g, w_out, attn_post_g, ffn_pre_g, w_gate, w_up, w_down, ffn_post_g):
    depth = w_in.shape[0]
    seq = x.shape[1]
    t = _tile(seq)
    assert seq % (COL_GROUPS * t) == 0 and t % LANES == 0 and x.shape[2] == D_MODEL

    cos, sin = _rope_tables(positions, ROPE_TILE if seq % ROPE_TILE == 0 else t)
    w_in_t = jnp.pad(jnp.swapaxes(w_in, 1, 2),
                     ((0, 0), (0, IN_ROWS - w_in.shape[2]), (0, 0))).astype(BF16)
    w_out_b = w_out.astype(BF16)
    w_gate_b = w_gate.astype(BF16)
    w_up_b = w_up.astype(BF16)
    w_down_b = w_down.astype(BF16)
    bias = jnp.pad(forget_bias.astype(F32), ((0, 0), (0, BF16_ROWS - N_HEADS_FOX)))[:, :, None]
    lam_vecs = jnp.pad(jnp.stack([lam_q1, lam_k1, lam_q2, lam_k2], axis=1).astype(F32),
                       ((0, 0), (0, SUBLANES - 4), (0, LANES - HEAD_DIM)))
    row = lambda v: v.reshape(1, -1).astype(F32)

    for l in range(depth):
        lam_init = 0.8 - 0.6 * math.exp(-0.3 * l)
        qd, kd, vd, qf, kf, vf = _project(x, row(attn_pre_g[l]), w_in_t[l], cos, sin, bias[l], t)
        oa = _diff_attention(lam_vecs[l], diff_sub_g[l].reshape(-1, 1).astype(F32), qd, kd, vd,
                             t, lam_init)
        ob = _fox_attention(qf, kf, vf, t)
        x = _mix_ffn(oa, ob, x, w_out_b[l, :DIFF_WIDTH], w_out_b[l, DIFF_WIDTH:],
                     row(attn_post_g[l]), row(ffn_pre_g[l]), w_gate_b[l], w_up_b[l], w_down_b[l],
                     row(ffn_post_g[l]), t)
    return x
```

```python
import functools
import math

import jax
import jax.numpy as jnp
from jax import lax
from jax.experimental import pallas as pl
from jax.experimental.pallas import tpu as pltpu

D_MODEL = 1024
HEAD_DIM = 64
N_HEADS_DIFF = 4
N_HEADS_FOX = 8
DIFF_WIDTH = N_HEADS_DIFF * 2 * HEAD_DIM
FOX_WIDTH = N_HEADS_FOX * HEAD_DIM
ROT_DIM = HEAD_DIM // 4
ROT_HALF = ROT_DIM // 2
ROPE_THETA = 500000.0
D_FF = 2816
EPS = 1e-6
NEG_INF = -1e30
LOG2E = 1.4426950408889634
Q_SCALE = HEAD_DIM ** -0.5 * LOG2E

LANES = 128
SUBLANES = 8
BF16_ROWS = 16
F_ROW = 3 * DIFF_WIDTH + 3 * FOX_WIDTH
IN_ROWS = F_ROW + BF16_ROWS
V_DIFF_ROWS = 2 * HEAD_DIM + BF16_ROWS
V_FOX_ROWS = HEAD_DIM + BF16_ROWS
VMEM_LIMIT = 56 * 1024 * 1024

F32 = jnp.float32
BF16 = jnp.bfloat16


def _tile(seq):
    return min(512, seq)


def _dot(a, b):
    return jnp.dot(a, b, preferred_element_type=F32)


def _rms(x, g):
    ms = jnp.mean(x * x, axis=-1, keepdims=True)
    return x * lax.rsqrt(ms + EPS) * g


def _split3(c):
    hi = c.astype(BF16).astype(F32)
    r = c - hi
    mid = r.astype(BF16).astype(F32)
    lo = r - mid
    return hi, mid, lo


ROPE_TILE = 2048


def _rope_kernel(pos_ref, invf_ref, cos_ref, sin_ref):
    ang = invf_ref[...] * pos_ref[...].astype(F32)
    cos_ref[...] = jnp.cos(ang)
    sin_ref[...] = jnp.sin(ang)


def _rope_tables(positions, tm):
    b, s = positions.shape
    inv_freq = 1.0 / (ROPE_THETA ** (jnp.arange(0, ROT_DIM, 2, dtype=F32) / ROT_DIM))
    spec = pl.BlockSpec((None, ROT_HALF, tm), lambda i, j: (i, 0, j))
    return pl.pallas_call(
        _rope_kernel,
        out_shape=(jax.ShapeDtypeStruct((b, ROT_HALF, s), F32),) * 2,
        grid=(b, s // tm),
        in_specs=[pl.BlockSpec((None, 1, tm), lambda i, j: (i, 0, j)),
                  pl.BlockSpec((ROT_HALF, 1), lambda i, j: (0, 0))],
        out_specs=(spec, spec),
        name="rope_tables",
    )(positions.reshape(b, 1, s), inv_freq.reshape(ROT_HALF, 1))


COL_GROUPS = 2


def _proj_kernel(x_ref, g_ref, wt_ref, cos_ref, sin_ref, bias_ref,
                 qd_ref, kd_ref, vd_ref, qf_ref, kf_ref, vf_ref, carry_ref):
    tm = x_ref.shape[0]
    tg = tm // COL_GROUPS
    groups = [slice(c0, c0 + tg) for c0 in range(0, tm, tg)]

    @pl.when(pl.program_id(1) == 0)
    def _():
        carry_ref[...] = jnp.zeros_like(carry_ref)

    h_ts = [_rms(x_ref[cs, :], g_ref[...]).T.astype(BF16) for cs in groups]
    ones_blk = jnp.ones((BF16_ROWS, tg), F32)

    def seg(gi, r0, n):
        return _dot(wt_ref[r0:r0 + n, :], h_ts[gi])

    def rot(y, cs):
        cosv = cos_ref[:, cs]
        sinv = sin_ref[:, cs]
        t1 = y[0:ROT_HALF]
        t2 = y[ROT_HALF:ROT_DIM]
        return jnp.concatenate([t1 * cosv - t2 * sinv, t2 * cosv + t1 * sinv, y[ROT_DIM:]], axis=0)

    def rot_head(y, cs):
        return jnp.concatenate([rot(y[0:HEAD_DIM], cs), rot(y[HEAD_DIM:], cs)], axis=0)

    for gi, cs in enumerate(groups):
        y = seg(gi, 0, DIFF_WIDTH)
        for hd in range(N_HEADS_DIFF):
            qd_ref[hd, :, cs] = (rot_head(y[hd * LANES:(hd + 1) * LANES], cs) * Q_SCALE).astype(BF16)
    for gi, cs in enumerate(groups):
        y = seg(gi, DIFF_WIDTH, DIFF_WIDTH)
        for hd in range(N_HEADS_DIFF):
            kd_ref[hd, cs, :] = rot_head(y[hd * LANES:(hd + 1) * LANES], cs).T.astype(BF16)
    for gi, cs in enumerate(groups):
        y = seg(gi, 2 * DIFF_WIDTH, DIFF_WIDTH)
        for hd in range(N_HEADS_DIFF):
            vd_ref[hd, gi] = jnp.concatenate(
                [y[hd * LANES:(hd + 1) * LANES], ones_blk], axis=0).astype(BF16)

    row16 = lax.broadcasted_iota(jnp.int32, (BF16_ROWS, tg), 0)
    tri = (lax.broadcasted_iota(jnp.int32, (tg, tg), 0)
           <= lax.broadcasted_iota(jnp.int32, (tg, tg), 1)).astype(BF16)
    carry = carry_ref[...]
    cs_all = []
    for gi in range(COL_GROUPS):
        z = seg(gi, F_ROW, BF16_ROWS) + bias_ref[...]
        log_f = -(jnp.maximum(-z, 0.0) + jnp.log1p(jnp.exp(-jnp.abs(z))))
        log_f = jnp.where(row16 < N_HEADS_FOX, log_f, 0.0)
        packed = jnp.concatenate(_split3(log_f), axis=0).astype(BF16)
        c3 = _dot(packed, tri)
        c = (c3[0:BF16_ROWS] + c3[BF16_ROWS:2 * BF16_ROWS] + c3[2 * BF16_ROWS:]
             + jnp.tile(carry, (1, tg // LANES)))
        carry = jnp.broadcast_to(c[:, tg - 1:tg], carry_ref.shape)
        cs_all.append(c)
    carry_ref[...] = carry

    row8 = lax.broadcasted_iota(jnp.int32, (SUBLANES, tg), 0)
    zpad = jnp.zeros((LANES - HEAD_DIM - SUBLANES, tg), F32)
    for gi, cs in enumerate(groups):
        yq = seg(gi, 3 * DIFF_WIDTH, FOX_WIDTH)
        yk = seg(gi, 3 * DIFF_WIDTH + FOX_WIDTH, FOX_WIDTH)
        yv = seg(gi, 3 * DIFF_WIDTH + 2 * FOX_WIDTH, FOX_WIDTH)
        c = cs_all[gi]
        for hf in range(N_HEADS_FOX):
            c_hi, c_mid, c_lo = _split3(
                jnp.broadcast_to(c[hf:hf + 1, :] * LOG2E, (SUBLANES, tg)))
            aug_q = jnp.where(row8 == 0, c_hi, jnp.where(row8 == 1, c_mid, jnp.where(
                row8 == 2, c_lo, jnp.where(row8 < 6, 1.0, 0.0))))
            aug_k = jnp.where(row8 < 3, 1.0, jnp.where(row8 == 3, -c_hi, jnp.where(
                row8 == 4, -c_mid, jnp.where(row8 == 5, -c_lo, 0.0))))
            feat = slice(hf * HEAD_DIM, (hf + 1) * HEAD_DIM)
            qf_ref[hf, :, cs] = jnp.concatenate(
                [yq[feat] * Q_SCALE, aug_q, zpad], axis=0).astype(BF16)
            kf_ref[hf, cs, :] = jnp.concatenate([yk[feat], aug_k, zpad], axis=0).T.astype(BF16)
            vf_ref[hf, gi] = jnp.concatenate([yv[feat], ones_blk], axis=0).astype(BF16)


def _project(x, g, wt, cos, sin, bias, t):
    b, s, _ = x.shape
    tm = COL_GROUPS * t
    tok = lambda i, j: (i, j, 0)
    const = lambda i, j: (0, 0)
    rope = pl.BlockSpec((None, ROT_HALF, tm), lambda i, j: (i, 0, j))
    sd = jax.ShapeDtypeStruct

    def specs(n, v_rows):
        return (
            (sd((b, n, LANES, s), BF16), pl.BlockSpec((None, n, LANES, tm), lambda i, j: (i, 0, 0, j))),
            (sd((b, n, s, LANES), BF16), pl.BlockSpec((None, n, tm, LANES), lambda i, j: (i, 0, j, 0))),
            (sd((b, n, s // t, v_rows, t), BF16),
             pl.BlockSpec((None, n, COL_GROUPS, v_rows, t), lambda i, j: (i, 0, j, 0, 0))),
        )

    outs = specs(N_HEADS_DIFF, V_DIFF_ROWS) + specs(N_HEADS_FOX, V_FOX_ROWS)
    return pl.pallas_call(
        _proj_kernel,
        out_shape=tuple(o[0] for o in outs),
        grid=(b, s // tm),
        in_specs=[pl.BlockSpec((None, tm, D_MODEL), tok),
                  pl.BlockSpec((1, D_MODEL), const),
                  pl.BlockSpec((IN_ROWS, D_MODEL), const, pipeline_mode=pl.Buffered(1)),
                  rope, rope,
                  pl.BlockSpec((BF16_ROWS, 1), const)],
        out_specs=tuple(o[1] for o in outs),
        scratch_shapes=[pltpu.VMEM((BF16_ROWS, LANES), F32)],
        compiler_params=pltpu.CompilerParams(
            dimension_semantics=("arbitrary", "arbitrary"), vmem_limit_bytes=VMEM_LIMIT),
        name="in_proj",
    )(x, g, wt, cos, sin, bias)


DIAG_BUF = 2


def _attend(qi, n_blocks, q_cur, q_next, k_at, v_at, s_buf, mc_buf, m_sc, acc_sc):
    n = len(q_cur)
    t = m_sc.shape[-1]
    half = t // 2
    lo, hi = slice(0, half), slice(half, t)
    m_sc[...] = jnp.full_like(m_sc, NEG_INF)
    acc_sc[...] = jnp.zeros_like(acc_sc)

    def stage_a(i, buf, q_ts, blk):
        s_t = _dot(k_at(i, blk), q_ts[i]())
        s_buf[buf, i, :, 0:t] = s_t
        mc_buf[buf, i] = jnp.max(s_t, axis=0, keepdims=True)

    def stage_bc(i, buf, blk):
        m_prev = m_sc[i]
        m_new = jnp.maximum(m_prev, mc_buf[buf, i])
        alpha = jnp.exp2(m_prev - m_new)
        p_t = jnp.exp2(s_buf[buf, i, :, 0:t] - m_new).astype(BF16)
        acc_sc[i] = alpha * acc_sc[i] + _dot(v_at(i, blk), p_t)
        m_sc[i] = m_new

    def stage_a_diag(i, q_ts, blk):
        q_t = q_ts[i]()
        k = k_at(i, blk)
        keep = (lax.broadcasted_iota(jnp.int32, (half, half), 1)
                >= lax.broadcasted_iota(jnp.int32, (half, half), 0))
        s_ll = jnp.where(keep, _dot(k[lo], q_t[:, lo]), NEG_INF)
        s_lh = _dot(k[lo], q_t[:, hi])
        s_hh = jnp.where(keep, _dot(k[hi], q_t[:, hi]), NEG_INF)
        s_buf[DIAG_BUF, i, lo, lo] = s_ll
        s_buf[DIAG_BUF, i, lo, hi] = s_lh
        s_buf[DIAG_BUF, i, hi, hi] = s_hh
        mc_buf[DIAG_BUF, i, :, lo] = jnp.max(s_ll, axis=0, keepdims=True)
        mc_buf[DIAG_BUF, i, :, hi] = jnp.maximum(jnp.max(s_lh, axis=0, keepdims=True),
                                                 jnp.max(s_hh, axis=0, keepdims=True))

    def stage_bc_diag(i, blk):
        m_prev = m_sc[i]
        m_new = jnp.maximum(m_prev, mc_buf[DIAG_BUF, i])
        alpha = jnp.exp2(m_prev - m_new)
        v_t = v_at(i, blk)
        p_lo = jnp.exp2(s_buf[DIAG_BUF, i, lo, lo] - m_new[:, lo]).astype(BF16)
        p_hi = jnp.exp2(s_buf[DIAG_BUF, i, :, hi] - m_new[:, hi]).astype(BF16)
        acc_sc[i, :, lo] = alpha[:, lo] * acc_sc[i, :, lo] + _dot(v_t[:, lo], p_lo)
        acc_sc[i, :, hi] = alpha[:, hi] * acc_sc[i, :, hi] + _dot(v_t, p_hi)
        m_sc[i] = m_new

    def overlap(a=None, bc=None):
        for i in range(n):
            if a is not None:
                buf, q_ts, blk = a
                stage_a_diag(i, q_ts, blk) if buf == DIAG_BUF else stage_a(i, buf, q_ts, blk)
            if bc is not None:
                buf, blk = bc
                stage_bc_diag(i, blk) if buf == DIAG_BUF else stage_bc(i, buf, blk)

    next_diag = (DIAG_BUF, q_next, jnp.minimum(qi + 1, n_blocks - 1))

    @pl.when(qi == 0)
    def _():
        overlap(a=(DIAG_BUF, q_cur, qi))
        overlap(bc=(DIAG_BUF, qi))
        overlap(a=next_diag)

    @pl.when(qi == 1)
    def _():
        overlap(a=(1, q_cur, 0), bc=(DIAG_BUF, qi))
        overlap(a=next_diag, bc=(1, 0))

    @pl.when(qi >= 2)
    def _():
        overlap(a=(1, q_cur, 0), bc=(DIAG_BUF, qi))
        overlap(a=(0, q_cur, 1), bc=(1, 0))

        def pair(ii):
            i = 2 * ii
            overlap(a=(1, q_cur, i), bc=(0, i - 1))
            overlap(a=(0, q_cur, i + 1), bc=(1, i))

        n_pairs = qi // 2 - 1

        def two_pairs(jj, carry):
            pair(1 + 2 * jj)
            pair(2 + 2 * jj)
            return carry

        lax.fori_loop(0, n_pairs // 2, two_pairs, 0)

        @pl.when(n_pairs % 2 == 1)
        def _():
            pair(n_pairs)

        @pl.when(qi % 2 == 1)
        def _():
            overlap(a=(1, q_cur, qi - 1), bc=(0, qi - 2))
            overlap(a=next_diag, bc=(1, qi - 1))

        @pl.when(qi % 2 == 0)
        def _():
            overlap(a=next_diag, bc=(0, qi - 1))


MAPS_PER_STEP = 4
DIFF_GROUP = MAPS_PER_STEP // 2
FOX_GROUP = MAPS_PER_STEP


def _next_q_block(n_blocks):
    return lambda i, h, j: (i, h, 0, jnp.minimum(j + 1, n_blocks - 1))


def _attend_scratch(v_rows, t):
    n = MAPS_PER_STEP
    return [pltpu.VMEM((DIAG_BUF + 1, n, t, t + LANES), F32),
            pltpu.VMEM((DIAG_BUF + 1, n, 1, t), F32),
            pltpu.VMEM((n, 1, t), F32), pltpu.VMEM((n, v_rows, t), F32)]


def _diff_attn_kernel(lam_ref, g_ref, q_ref, qn_ref, k_ref, v_ref, o_ref,
                      s_buf, mc_buf, m_sc, acc_sc, *, lam_init):
    t = q_ref.shape[2]
    qi = pl.program_id(2)
    zero = jnp.zeros((HEAD_DIM, t), BF16)

    def maps_of(ref):
        q_ts = []
        for hd in range(DIFF_GROUP):
            q_ts.append(lambda hd=hd: jnp.concatenate([ref[hd, 0:HEAD_DIM, :], zero], axis=0))
            q_ts.append(lambda hd=hd: jnp.concatenate([zero, ref[hd, HEAD_DIM:, :]], axis=0))
        return q_ts

    k_at = lambda i, j: k_ref[i // 2, pl.ds(pl.multiple_of(j * t, t), t), :]
    v_at = lambda i, j: v_ref[i // 2, j]
    _attend(qi, v_ref.shape[1], maps_of(q_ref), maps_of(qn_ref), k_at, v_at,
            s_buf, mc_buf, m_sc, acc_sc)

    lam = (jnp.exp(jnp.sum(lam_ref[0:1, :] * lam_ref[1:2, :], axis=1, keepdims=True))
           - jnp.exp(jnp.sum(lam_ref[2:3, :] * lam_ref[3:4, :], axis=1, keepdims=True)) + lam_init)
    nv = 2 * HEAD_DIM
    outs = []
    for hd in range(DIFF_GROUP):
        a0 = acc_sc[2 * hd]
        a1 = acc_sc[2 * hd + 1]
        o_t = a0[0:nv] / a0[nv:nv + 1] - lam * (a1[0:nv] / a1[nv:nv + 1])
        ms = jnp.mean(o_t * o_t, axis=0, keepdims=True)
        outs.append(o_t * lax.rsqrt(ms + EPS) * g_ref[...] * (1.0 - lam_init))
    o_ref[...] = jnp.concatenate(outs, axis=0).T.astype(o_ref.dtype)


def _diff_attention(lam_vecs, g_col, qd, kd, vd, t, lam_init):
    b, nh, s, _ = kd.shape
    nkb = s // t
    g = DIFF_GROUP
    return pl.pallas_call(
        functools.partial(_diff_attn_kernel, lam_init=lam_init),
        out_shape=jax.ShapeDtypeStruct((b, s, DIFF_WIDTH), BF16),
        grid=(b, nh // g, nkb),
        in_specs=[pl.BlockSpec((SUBLANES, LANES), lambda i, h, j: (0, 0)),
                  pl.BlockSpec((LANES, 1), lambda i, h, j: (0, 0)),
                  pl.BlockSpec((None, g, LANES, t), lambda i, h, j: (i, h, 0, j)),
                  pl.BlockSpec((None, g, LANES, t), _next_q_block(nkb)),
                  pl.BlockSpec((None, g, s, LANES), lambda i, h, j: (i, h, 0, 0)),
                  pl.BlockSpec((None, g, nkb, V_DIFF_ROWS, t), lambda i, h, j: (i, h, 0, 0, 0))],
        out_specs=pl.BlockSpec((None, t, g * 2 * HEAD_DIM), lambda i, h, j: (i, j, h)),
        scratch_shapes=_attend_scratch(V_DIFF_ROWS, t),
        compiler_params=pltpu.CompilerParams(
            dimension_semantics=("arbitrary",) * 3, vmem_limit_bytes=VMEM_LIMIT),
        name="diff_attention",
    )(lam_vecs, g_col, qd, qd, kd, vd)


def _fox_attn_kernel(q_ref, qn_ref, k_ref, v_ref, o_ref, s_buf, mc_buf, m_sc, acc_sc):
    t = q_ref.shape[2]
    qi = pl.program_id(2)
    maps_of = lambda ref: tuple((lambda hh=hh: ref[hh]) for hh in range(FOX_GROUP))
    k_at = lambda i, j: k_ref[i, pl.ds(pl.multiple_of(j * t, t), t), :]
    v_at = lambda i, j: v_ref[i, j]
    _attend(qi, v_ref.shape[1], maps_of(q_ref), maps_of(qn_ref), k_at, v_at,
            s_buf, mc_buf, m_sc, acc_sc)

    outs = []
    for hh in range(FOX_GROUP):
        a = acc_sc[hh]
        outs.append(a[0:HEAD_DIM] / a[HEAD_DIM:HEAD_DIM + 1])
    o_ref[...] = jnp.concatenate(outs, axis=0).T.astype(o_ref.dtype)


def _fox_attention(qf, kf, vf, t):
    b, nh, s, _ = kf.shape
    nkb = s // t
    g = FOX_GROUP
    return pl.pallas_call(
        _fox_attn_kernel,
        out_shape=jax.ShapeDtypeStruct((b, s, FOX_WIDTH), BF16),
        grid=(b, nh // g, nkb),
        in_specs=[pl.BlockSpec((None, g, LANES, t), lambda i, h, j: (i, h, 0, j)),
                  pl.BlockSpec((None, g, LANES, t), _next_q_block(nkb)),
                  pl.BlockSpec((None, g, s, LANES), lambda i, h, j: (i, h, 0, 0)),
                  pl.BlockSpec((None, g, nkb, V_FOX_ROWS, t), lambda i, h, j: (i, h, 0, 0, 0))],
        out_specs=pl.BlockSpec((None, t, g * HEAD_DIM), lambda i, h, j: (i, j, h)),
        scratch_shapes=_attend_scratch(V_FOX_ROWS, t),
        compiler_params=pltpu.CompilerParams(
            dimension_semantics=("arbitrary",) * 3, vmem_limit_bytes=VMEM_LIMIT),
        name="fox_attention",
    )(qf, qf, kf, vf)


MXU_TILE = 256
FF_CHUNKS = ((0, 6 * MXU_TILE), (6 * MXU_TILE, D_FF))
ROW_SPLIT = 2


def _mix_ffn_kernel(oa_ref, ob_ref, x_ref, wa_ref, wb_ref, gmix_ref, gpre_ref,
                    wg_ref, wu_ref, wd_ref, gpost_ref, o_ref):
    tm = x_ref.shape[0]
    halves = [slice(r0, r0 + tm // ROW_SPLIT) for r0 in range(0, tm, tm // ROW_SPLIT)]
    xs, hs = [], []
    for rs in halves:
        y = _dot(oa_ref[rs, :], wa_ref[...]) + _dot(ob_ref[rs, :], wb_ref[...])
        x = x_ref[rs, :] + _rms(y, gmix_ref[...])
        xs.append(x)
        hs.append(_rms(x, gpre_ref[...]).astype(BF16))
    ys = [None] * len(halves)
    for c0, c1 in FF_CHUNKS:
        for k in range(len(halves)):
            gate = _dot(hs[k], wg_ref[:, c0:c1])
            up = _dot(hs[k], wu_ref[:, c0:c1])
            act = (gate * jax.nn.sigmoid(gate) * up).astype(BF16)
            part = _dot(act, wd_ref[c0:c1, :])
            ys[k] = part if ys[k] is None else ys[k] + part
    for k, rs in enumerate(halves):
        o_ref[rs, :] = xs[k] + _rms(ys[k], gpost_ref[...])


def _mix_ffn(oa, ob, x, wa, wb, gmix, gpre, wg, wu, wd, gpost, tm):
    b, s, _ = x.shape
    tok = lambda i, j: (i, j, 0)
    const = lambda i, j: (0, 0)
    resident = lambda shape: pl.BlockSpec(shape, const, pipeline_mode=pl.Buffered(1))
    gain = pl.BlockSpec((1, D_MODEL), const)
    return pl.pallas_call(
        _mix_ffn_kernel,
        out_shape=jax.ShapeDtypeStruct(x.shape, F32),
        grid=(b, s // tm),
        in_specs=[pl.BlockSpec((None, tm, DIFF_WIDTH), tok),
                  pl.BlockSpec((None, tm, FOX_WIDTH), tok),
                  pl.BlockSpec((None, tm, D_MODEL), tok),
                  resident((DIFF_WIDTH, D_MODEL)), resident((FOX_WIDTH, D_MODEL)), gain, gain,
                  resident((D_MODEL, D_FF)), resident((D_MODEL, D_FF)), resident((D_FF, D_MODEL)),
                  gain],
        out_specs=pl.BlockSpec((None, tm, D_MODEL), tok),
        compiler_params=pltpu.CompilerParams(
            dimension_semantics=("arbitrary", "arbitrary"), vmem_limit_bytes=VMEM_LIMIT),
        name="mix_ffn",
    )(oa, ob, x, wa, wb, gmix, gpre, wg, wu, wd, gpost)


def kernel(x, positions, attn_pre_g, w_in, forget_bias, lam_q1, lam_k1, lam_q2, lam_k2,
           diff_sub_g, w_out, attn_post_g, ffn_pre_g, w_gate, w_up, w_down, ffn_post_g):
    depth = w_in.shape[0]
    seq = x.shape[1]
    t = _tile(seq)
    assert seq % (COL_GROUPS * t) == 0 and t % LANES == 0 and x.shape[2] == D_MODEL

    cos, sin = _rope_tables(positions, ROPE_TILE if seq % ROPE_TILE == 0 else t)
    w_in_t = jnp.pad(jnp.swapaxes(w_in, 1, 2),
                     ((0, 0), (0, IN_ROWS - w_in.shape[2]), (0, 0))).astype(BF16)
    w_out_b = w_out.astype(BF16)
    w_gate_b = w_gate.astype(BF16)
    w_up_b = w_up.astype(BF16)
    w_down_b = w_down.astype(BF16)
    bias = jnp.pad(forget_bias.astype(F32), ((0, 0), (0, BF16_ROWS - N_HEADS_FOX)))[:, :, None]
    lam_vecs = jnp.pad(jnp.stack([lam_q1, lam_k1, lam_q2, lam_k2], axis=1).astype(F32),
                       ((0, 0), (0, SUBLANES - 4), (0, LANES - HEAD_DIM)))
    row = lambda v: v.reshape(1, -1).astype(F32)

    for l in range(depth):
        lam_init = 0.8 - 0.6 * math.exp(-0.3 * l)
        qd, kd, vd, qf, kf, vf = _project(x, row(attn_pre_g[l]), w_in_t[l], cos, sin, bias[l], t)
        oa = _diff_attention(lam_vecs[l], diff_sub_g[l].reshape(-1, 1).astype(F32), qd, kd, vd,
                             t, lam_init)
        ob = _fox_attention(qf, kf, vf, t)
        x = _mix_ffn(oa, ob, x, w_out_b[l, :DIFF_WIDTH], w_out_b[l, DIFF_WIDTH:],
                     row(attn_post_g[l]), row(ffn_pre_g[l]), w_gate_b[l], w_up_b[l], w_down_b[l],
                     row(ffn_post_g[l]), t)
    return x
```

```python
import functools
import math

import jax
import jax.numpy as jnp
from jax import lax
from jax.experimental import pallas as pl
from jax.experimental.pallas import tpu as pltpu

D_MODEL = 1024
HEAD_DIM = 64
N_HEADS_DIFF = 4
N_HEADS_FOX = 8
DIFF_WIDTH = N_HEADS_DIFF * 2 * HEAD_DIM
FOX_WIDTH = N_HEADS_FOX * HEAD_DIM
ROT_DIM = HEAD_DIM // 4
ROT_HALF = ROT_DIM // 2
ROPE_THETA = 500000.0
D_FF = 2816
EPS = 1e-6
NEG_INF = -1e30
LOG2E = 1.4426950408889634
Q_SCALE = HEAD_DIM ** -0.5 * LOG2E

LANES = 128
SUBLANES = 8
BF16_ROWS = 16
F_ROW = 3 * DIFF_WIDTH + 3 * FOX_WIDTH
IN_ROWS = F_ROW + BF16_ROWS
V_DIFF_ROWS = 2 * HEAD_DIM + BF16_ROWS
V_FOX_ROWS = HEAD_DIM + BF16_ROWS
VMEM_LIMIT = 56 * 1024 * 1024

F32 = jnp.float32
BF16 = jnp.bfloat16


def _tile(seq):
    return min(512, seq)


def _dot(a, b):
    return jnp.dot(a, b, preferred_element_type=F32)


def _rms(x, g):
    ms = jnp.mean(x * x, axis=-1, keepdims=True)
    return x * lax.rsqrt(ms + EPS) * g


def _split3(c):
    hi = c.astype(BF16).astype(F32)
    r = c - hi
    mid = r.astype(BF16).astype(F32)
    lo = r - mid
    return hi, mid, lo


ROPE_TILE = 2048


def _rope_kernel(pos_ref, invf_ref, cos_ref, sin_ref):
    ang = invf_ref[...] * pos_ref[...].astype(F32)
    cos_ref[...] = jnp.cos(ang)
    sin_ref[...] = jnp.sin(ang)


def _rope_tables(positions, tm):
    b, s = positions.shape
    inv_freq = 1.0 / (ROPE_THETA ** (jnp.arange(0, ROT_DIM, 2, dtype=F32) / ROT_DIM))
    spec = pl.BlockSpec((None, ROT_HALF, tm), lambda i, j: (i, 0, j))
    return pl.pallas_call(
        _rope_kernel,
        out_shape=(jax.ShapeDtypeStruct((b, ROT_HALF, s), F32),) * 2,
        grid=(b, s // tm),
        in_specs=[pl.BlockSpec((None, 1, tm), lambda i, j: (i, 0, j)),
                  pl.BlockSpec((ROT_HALF, 1), lambda i, j: (0, 0))],
        out_specs=(spec, spec),
        name="rope_tables",
    )(positions.reshape(b, 1, s), inv_freq.reshape(ROT_HALF, 1))


COL_GROUPS = 2


def _proj_kernel(x_ref, g_ref, wt_ref, cos_ref, sin_ref, bias_ref,
                 qd_ref, kd_ref, vd_ref, qf_ref, kf_ref, vf_ref, carry_ref):
    tm = x_ref.shape[0]
    tg = tm // COL_GROUPS
    groups = [slice(c0, c0 + tg) for c0 in range(0, tm, tg)]

    @pl.when(pl.program_id(1) == 0)
    def _():
        carry_ref[...] = jnp.zeros_like(carry_ref)

    h_ts = [_rms(x_ref[cs, :], g_ref[...]).T.astype(BF16) for cs in groups]
    ones_blk = jnp.ones((BF16_ROWS, tg), F32)

    def seg(gi, r0, n):
        return _dot(wt_ref[r0:r0 + n, :], h_ts[gi])

    def rot(y, cs):
        cosv = cos_ref[:, cs]
        sinv = sin_ref[:, cs]
        t1 = y[0:ROT_HALF]
        t2 = y[ROT_HALF:ROT_DIM]
        return jnp.concatenate([t1 * cosv - t2 * sinv, t2 * cosv + t1 * sinv, y[ROT_DIM:]], axis=0)

    def rot_head(y, cs):
        return jnp.concatenate([rot(y[0:HEAD_DIM], cs), rot(y[HEAD_DIM:], cs)], axis=0)

    for gi, cs in enumerate(groups):
        y = seg(gi, 0, DIFF_WIDTH)
        for hd in range(N_HEADS_DIFF):
            qd_ref[hd, :, cs] = (rot_head(y[hd * LANES:(hd + 1) * LANES], cs) * Q_SCALE).astype(BF16)
    for gi, cs in enumerate(groups):
        y = seg(gi, DIFF_WIDTH, DIFF_WIDTH)
        for hd in range(N_HEADS_DIFF):
            kd_ref[hd, cs, :] = rot_head(y[hd * LANES:(hd + 1) * LANES], cs).T.astype(BF16)
    for gi, cs in enumerate(groups):
        y = seg(gi, 2 * DIFF_WIDTH, DIFF_WIDTH)
        for hd in range(N_HEADS_DIFF):
            vd_ref[hd, gi] = jnp.concatenate(
                [y[hd * LANES:(hd + 1) * LANES], ones_blk], axis=0).astype(BF16)

    row16 = lax.broadcasted_iota(jnp.int32, (BF16_ROWS, tg), 0)
    tri = (lax.broadcasted_iota(jnp.int32, (tg, tg), 0)
           <= lax.broadcasted_iota(jnp.int32, (tg, tg), 1)).astype(BF16)
    carry = carry_ref[...]
    cs_all = []
    for gi in range(COL_GROUPS):
        z = seg(gi, F_ROW, BF16_ROWS) + bias_ref[...]
        log_f = -(jnp.maximum(-z, 0.0) + jnp.log1p(jnp.exp(-jnp.abs(z))))
        log_f = jnp.where(row16 < N_HEADS_FOX, log_f, 0.0)
        packed = jnp.concatenate(_split3(log_f), axis=0).astype(BF16)
        c3 = _dot(packed, tri)
        c = (c3[0:BF16_ROWS] + c3[BF16_ROWS:2 * BF16_ROWS] + c3[2 * BF16_ROWS:]
             + jnp.tile(carry, (1, tg // LANES)))
        carry = jnp.broadcast_to(c[:, tg - 1:tg], carry_ref.shape)
        cs_all.append(c)
    carry_ref[...] = carry

    row8 = lax.broadcasted_iota(jnp.int32, (SUBLANES, tg), 0)
    zpad = jnp.zeros((LANES - HEAD_DIM - SUBLANES, tg), F32)
    for gi, cs in enumerate(groups):
        yq = seg(gi, 3 * DIFF_WIDTH, FOX_WIDTH)
        yk = seg(gi, 3 * DIFF_WIDTH + FOX_WIDTH, FOX_WIDTH)
        yv = seg(gi, 3 * DIFF_WIDTH + 2 * FOX_WIDTH, FOX_WIDTH)
        c = cs_all[gi]
        for hf in range(N_HEADS_FOX):
            c_hi, c_mid, c_lo = _split3(
                jnp.broadcast_to(c[hf:hf + 1, :] * LOG2E, (SUBLANES, tg)))
            aug_q = jnp.where(row8 == 0, c_hi, jnp.where(row8 == 1, c_mid, jnp.where(
                row8 == 2, c_lo, jnp.where(row8 < 6, 1.0, 0.0))))
            aug_k = jnp.where(row8 < 3, 1.0, jnp.where(row8 == 3, -c_hi, jnp.where(
                row8 == 4, -c_mid, jnp.where(row8 == 5, -c_lo, 0.0))))
            feat = slice(hf * HEAD_DIM, (hf + 1) * HEAD_DIM)
            qf_ref[hf, :, cs] = jnp.concatenate(
                [yq[feat] * Q_SCALE, aug_q, zpad], axis=0).astype(BF16)
            kf_ref[hf, cs, :] = jnp.concatenate([yk[feat], aug_k, zpad], axis=0).T.astype(BF16)
            vf_ref[hf, gi] = jnp.concatenate([yv[feat], ones_blk], axis=0).astype(BF16)


def _project(x, g, wt, cos, sin, bias, t):
    b, s, _ = x.shape
    tm = COL_GROUPS * t
    tok = lambda i, j: (i, j, 0)
    const = lambda i, j: (0, 0)
    rope = pl.BlockSpec((None, ROT_HALF, tm), lambda i, j: (i, 0, j))
    sd = jax.ShapeDtypeStruct

    def specs(n, v_rows):
        return (
            (sd((b, n, LANES, s), BF16), pl.BlockSpec((None, n, LANES, tm), lambda i, j: (i, 0, 0, j))),
            (sd((b, n, s, LANES), BF16), pl.BlockSpec((None, n, tm, LANES), lambda i, j: (i, 0, j, 0))),
            (sd((b, n, s // t, v_rows, t), BF16),
             pl.BlockSpec((None, n, COL_GROUPS, v_rows, t), lambda i, j: (i, 0, j, 0, 0))),
        )

    outs = specs(N_HEADS_DIFF, V_DIFF_ROWS) + specs(N_HEADS_FOX, V_FOX_ROWS)
    return pl.pallas_call(
        _proj_kernel,
        out_shape=tuple(o[0] for o in outs),
        grid=(b, s // tm),
        in_specs=[pl.BlockSpec((None, tm, D_MODEL), tok),
                  pl.BlockSpec((1, D_MODEL), const),
                  pl.BlockSpec((IN_ROWS, D_MODEL), const, pipeline_mode=pl.Buffered(1)),
                  rope, rope,
                  pl.BlockSpec((BF16_ROWS, 1), const)],
        out_specs=tuple(o[1] for o in outs),
        scratch_shapes=[pltpu.VMEM((BF16_ROWS, LANES), F32)],
        compiler_params=pltpu.CompilerParams(
            dimension_semantics=("arbitrary", "arbitrary"), vmem_limit_bytes=VMEM_LIMIT),
        name="in_proj",
    )(x, g, wt, cos, sin, bias)


DIAG_BUF = 2
SCORE_PAD = 2 * LANES


def _attend(qi, n_blocks, q_cur, q_next, k_at, v_at, s_buf, mc_buf, m_sc, acc_sc):
    n = len(q_cur)
    t = m_sc.shape[-1]
    half = t // 2
    lo, hi = slice(0, half), slice(half, t)
    m_sc[...] = jnp.full_like(m_sc, NEG_INF)
    acc_sc[...] = jnp.zeros_like(acc_sc)

    def stage_a(i, buf, q_ts, blk):
        s_t = _dot(k_at(i, blk), q_ts[i]())
        s_buf[buf, i, :, 0:t] = s_t
        mc_buf[buf, i] = jnp.max(s_t, axis=0, keepdims=True)

    def stage_bc(i, buf, blk):
        m_prev = m_sc[i]
        m_new = jnp.maximum(m_prev, mc_buf[buf, i])
        alpha = jnp.exp2(m_prev - m_new)
        p_t = jnp.exp2(s_buf[buf, i, :, 0:t] - m_new).astype(BF16)
        acc_sc[i] = alpha * acc_sc[i] + _dot(v_at(i, blk), p_t)
        m_sc[i] = m_new

    def stage_a_diag(i, q_ts, blk):
        q_t = q_ts[i]()
        k = k_at(i, blk)
        keep = (lax.broadcasted_iota(jnp.int32, (half, half), 1)
                >= lax.broadcasted_iota(jnp.int32, (half, half), 0))
        s_ll = jnp.where(keep, _dot(k[lo], q_t[:, lo]), NEG_INF)
        s_lh = _dot(k[lo], q_t[:, hi])
        s_hh = jnp.where(keep, _dot(k[hi], q_t[:, hi]), NEG_INF)
        s_buf[DIAG_BUF, i, lo, lo] = s_ll
        s_buf[DIAG_BUF, i, lo, hi] = s_lh
        s_buf[DIAG_BUF, i, hi, hi] = s_hh
        mc_buf[DIAG_BUF, i, :, lo] = jnp.max(s_ll, axis=0, keepdims=True)
        mc_buf[DIAG_BUF, i, :, hi] = jnp.maximum(jnp.max(s_lh, axis=0, keepdims=True),
                                                 jnp.max(s_hh, axis=0, keepdims=True))

    def stage_bc_diag(i, blk):
        m_prev = m_sc[i]
        m_new = jnp.maximum(m_prev, mc_buf[DIAG_BUF, i])
        alpha = jnp.exp2(m_prev - m_new)
        v_t = v_at(i, blk)
        p_lo = jnp.exp2(s_buf[DIAG_BUF, i, lo, lo] - m_new[:, lo]).astype(BF16)
        p_hi = jnp.exp2(s_buf[DIAG_BUF, i, :, hi] - m_new[:, hi]).astype(BF16)
        acc_sc[i, :, lo] = alpha[:, lo] * acc_sc[i, :, lo] + _dot(v_t[:, lo], p_lo)
        acc_sc[i, :, hi] = alpha[:, hi] * acc_sc[i, :, hi] + _dot(v_t, p_hi)
        m_sc[i] = m_new

    def overlap(a=None, bc=None):
        for i in range(n):
            if a is not None:
                buf, q_ts, blk = a
                stage_a_diag(i, q_ts, blk) if buf == DIAG_BUF else stage_a(i, buf, q_ts, blk)
            if bc is not None:
                buf, blk = bc
                stage_bc_diag(i, blk) if buf == DIAG_BUF else stage_bc(i, buf, blk)

    next_diag = (DIAG_BUF, q_next, jnp.minimum(qi + 1, n_blocks - 1))

    @pl.when(qi == 0)
    def _():
        overlap(a=(DIAG_BUF, q_cur, qi))
        overlap(bc=(DIAG_BUF, qi))
        overlap(a=next_diag)

    @pl.when(qi == 1)
    def _():
        overlap(a=(1, q_cur, 0), bc=(DIAG_BUF, qi))
        overlap(a=next_diag, bc=(1, 0))

    @pl.when(qi >= 2)
    def _():
        overlap(a=(1, q_cur, 0), bc=(DIAG_BUF, qi))
        overlap(a=(0, q_cur, 1), bc=(1, 0))

        def pair(ii):
            i = 2 * ii
            overlap(a=(1, q_cur, i), bc=(0, i - 1))
            overlap(a=(0, q_cur, i + 1), bc=(1, i))

        n_pairs = qi // 2 - 1

        def two_pairs(jj, carry):
            pair(1 + 2 * jj)
            pair(2 + 2 * jj)
            return carry

        lax.fori_loop(0, n_pairs // 2, two_pairs, 0)

        @pl.when(n_pairs % 2 == 1)
        def _():
            pair(n_pairs)

        @pl.when(qi % 2 == 1)
        def _():
            overlap(a=(1, q_cur, qi - 1), bc=(0, qi - 2))
            overlap(a=next_diag, bc=(1, qi - 1))

        @pl.when(qi % 2 == 0)
        def _():
            overlap(a=next_diag, bc=(0, qi - 1))


MAPS_PER_STEP = 4
DIFF_GROUP = MAPS_PER_STEP // 2
FOX_GROUP = MAPS_PER_STEP


def _next_q_block(n_blocks):
    return lambda i, h, j: (i, h, 0, jnp.minimum(j + 1, n_blocks - 1))


def _attend_scratch(v_rows, t):
    n = MAPS_PER_STEP
    return [pltpu.VMEM((DIAG_BUF + 1, n, t, t + SCORE_PAD), F32),
            pltpu.VMEM((DIAG_BUF + 1, n, 1, t), F32),
            pltpu.VMEM((n, 1, t), F32), pltpu.VMEM((n, v_rows, t), F32)]


def _diff_attn_kernel(lam_ref, g_ref, q_ref, qn_ref, k_ref, v_ref, o_ref,
                      s_buf, mc_buf, m_sc, acc_sc, *, lam_init):
    t = q_ref.shape[2]
    qi = pl.program_id(2)
    zero = jnp.zeros((HEAD_DIM, t), BF16)

    def maps_of(ref):
        q_ts = []
        for hd in range(DIFF_GROUP):
            q_ts.append(lambda hd=hd: jnp.concatenate([ref[hd, 0:HEAD_DIM, :], zero], axis=0))
            q_ts.append(lambda hd=hd: jnp.concatenate([zero, ref[hd, HEAD_DIM:, :]], axis=0))
        return q_ts

    k_at = lambda i, j: k_ref[i // 2, pl.ds(pl.multiple_of(j * t, t), t), :]
    v_at = lambda i, j: v_ref[i // 2, j]
    _attend(qi, v_ref.shape[1], maps_of(q_ref), maps_of(qn_ref), k_at, v_at,
            s_buf, mc_buf, m_sc, acc_sc)

    lam = (jnp.exp(jnp.sum(lam_ref[0:1, :] * lam_ref[1:2, :], axis=1, keepdims=True))
           - jnp.exp(jnp.sum(lam_ref[2:3, :] * lam_ref[3:4, :], axis=1, keepdims=True)) + lam_init)
    nv = 2 * HEAD_DIM
    outs = []
    for hd in range(DIFF_GROUP):
        a0 = acc_sc[2 * hd]
        a1 = acc_sc[2 * hd + 1]
        o_t = a0[0:nv] / a0[nv:nv + 1] - lam * (a1[0:nv] / a1[nv:nv + 1])
        ms = jnp.mean(o_t * o_t, axis=0, keepdims=True)
        outs.append(o_t * lax.rsqrt(ms + EPS) * g_ref[...] * (1.0 - lam_init))
    o_ref[...] = jnp.concatenate(outs, axis=0).T.astype(o_ref.dtype)


def _diff_attention(lam_vecs, g_col, qd, kd, vd, t, lam_init):
    b, nh, s, _ = kd.shape
    nkb = s // t
    g = DIFF_GROUP
    return pl.pallas_call(
        functools.partial(_diff_attn_kernel, lam_init=lam_init),
        out_shape=jax.ShapeDtypeStruct((b, s, DIFF_WIDTH), BF16),
        grid=(b, nh // g, nkb),
        in_specs=[pl.BlockSpec((SUBLANES, LANES), lambda i, h, j: (0, 0)),
                  pl.BlockSpec((LANES, 1), lambda i, h, j: (0, 0)),
                  pl.BlockSpec((None, g, LANES, t), lambda i, h, j: (i, h, 0, j)),
                  pl.BlockSpec((None, g, LANES, t), _next_q_block(nkb)),
                  pl.BlockSpec((None, g, s, LANES), lambda i, h, j: (i, h, 0, 0)),
                  pl.BlockSpec((None, g, nkb, V_DIFF_ROWS, t), lambda i, h, j: (i, h, 0, 0, 0))],
        out_specs=pl.BlockSpec((None, t, g * 2 * HEAD_DIM), lambda i, h, j: (i, j, h)),
        scratch_shapes=_attend_scratch(V_DIFF_ROWS, t),
        compiler_params=pltpu.CompilerParams(
            dimension_semantics=("arbitrary",) * 3, vmem_limit_bytes=VMEM_LIMIT),
        name="diff_attention",
    )(lam_vecs, g_col, qd, qd, kd, vd)


def _fox_attn_kernel(q_ref, qn_ref, k_ref, v_ref, o_ref, s_buf, mc_buf, m_sc, acc_sc):
    t = q_ref.shape[2]
    qi = pl.program_id(2)
    maps_of = lambda ref: tuple((lambda hh=hh: ref[hh]) for hh in range(FOX_GROUP))
    k_at = lambda i, j: k_ref[i, pl.ds(pl.multiple_of(j * t, t), t), :]
    v_at = lambda i, j: v_ref[i, j]
    _attend(qi, v_ref.shape[1], maps_of(q_ref), maps_of(qn_ref), k_at, v_at,
            s_buf, mc_buf, m_sc, acc_sc)

    outs = []
    for hh in range(FOX_GROUP):
        a = acc_sc[hh]
        outs.append(a[0:HEAD_DIM] / a[HEAD_DIM:HEAD_DIM + 1])
    o_ref[...] = jnp.concatenate(outs, axis=0).T.astype(o_ref.dtype)


def _fox_attention(qf, kf, vf, t):
    b, nh, s, _ = kf.shape
    nkb = s // t
    g = FOX_GROUP
    return pl.pallas_call(
        _fox_attn_kernel,
        out_shape=jax.ShapeDtypeStruct((b, s, FOX_WIDTH), BF16),
        grid=(b, nh // g, nkb),
        in_specs=[pl.BlockSpec((None, g, LANES, t), lambda i, h, j: (i, h, 0, j)),
                  pl.BlockSpec((None, g, LANES, t), _next_q_block(nkb)),
                  pl.BlockSpec((None, g, s, LANES), lambda i, h, j: (i, h, 0, 0)),
                  pl.BlockSpec((None, g, nkb, V_FOX_ROWS, t), lambda i, h, j: (i, h, 0, 0, 0))],
        out_specs=pl.BlockSpec((None, t, g * HEAD_DIM), lambda i, h, j: (i, j, h)),
        scratch_shapes=_attend_scratch(V_FOX_ROWS, t),
        compiler_params=pltpu.CompilerParams(
            dimension_semantics=("arbitrary",) * 3, vmem_limit_bytes=VMEM_LIMIT),
        name="fox_attention",
    )(qf, qf, kf, vf)


MXU_TILE = 256
FF_CHUNKS = ((0, 6 * MXU_TILE), (6 * MXU_TILE, D_FF))
ROW_SPLIT = 2


def _mix_ffn_kernel(oa_ref, ob_ref, x_ref, wa_ref, wb_ref, gmix_ref, gpre_ref,
                    wg_ref, wu_ref, wd_ref, gpost_ref, o_ref):
    tm = x_ref.shape[0]
    halves = [slice(r0, r0 + tm // ROW_SPLIT) for r0 in range(0, tm, tm // ROW_SPLIT)]
    xs, hs = [], []
    for rs in halves:
        y = _dot(oa_ref[rs, :], wa_ref[...]) + _dot(ob_ref[rs, :], wb_ref[...])
        x = x_ref[rs, :] + _rms(y, gmix_ref[...])
        xs.append(x)
        hs.append(_rms(x, gpre_ref[...]).astype(BF16))
    ys = [None] * len(halves)
    for c0, c1 in FF_CHUNKS:
        for k in range(len(halves)):
            gate = _dot(hs[k], wg_ref[:, c0:c1])
            up = _dot(hs[k], wu_ref[:, c0:c1])
            act = (gate * jax.nn.sigmoid(gate) * up).astype(BF16)
            part = _dot(act, wd_ref[c0:c1, :])
            ys[k] = part if ys[k] is None else ys[k] + part
    for k, rs in enumerate(halves):
        o_ref[rs, :] = xs[k] + _rms(ys[k], gpost_ref[...])


def _mix_ffn(oa, ob, x, wa, wb, gmix, gpre, wg, wu, wd, gpost, tm):
    b, s, _ = x.shape
    tok = lambda i, j: (i, j, 0)
    const = lambda i, j: (0, 0)
    resident = lambda shape: pl.BlockSpec(shape, const, pipeline_mode=pl.Buffered(1))
    gain = pl.BlockSpec((1, D_MODEL), const)
    return pl.pallas_call(
        _mix_ffn_kernel,
        out_shape=jax.ShapeDtypeStruct(x.shape, F32),
        grid=(b, s // tm),
        in_specs=[pl.BlockSpec((None, tm, DIFF_WIDTH), tok),
                  pl.BlockSpec((None, tm, FOX_WIDTH), tok),
                  pl.BlockSpec((None, tm, D_MODEL), tok),
                  resident((DIFF_WIDTH, D_MODEL)), resident((FOX_WIDTH, D_MODEL)), gain, gain,
                  resident((D_MODEL, D_FF)), resident((D_MODEL, D_FF)), resident((D_FF, D_MODEL)),
                  gain],
        out_specs=pl.BlockSpec((None, tm, D_MODEL), tok),
        compiler_params=pltpu.CompilerParams(
            dimension_semantics=("arbitrary", "arbitrary"), vmem_limit_bytes=VMEM_LIMIT),
        name="mix_ffn",
    )(oa, ob, x, wa, wb, gmix, gpre, wg, wu, wd, gpost)


def kernel(x, positions, attn_pre_g, w_in, forget_bias, lam_q1, lam_k1, lam_q2, lam_k2,
           diff_sub_g, w_out, attn_post_g, ffn_pre_g, w_gate, w_up, w_down, ffn_post_g):
    depth = w_in.shape[0]
    seq = x.shape[1]
    t = _tile(seq)
    assert seq % (COL_GROUPS * t) == 0 and t % LANES == 0 and x.shape[2] == D_MODEL

    cos, sin = _rope_tables(positions, ROPE_TILE if seq % ROPE_TILE == 0 else t)
    w_in_t = jnp.pad(jnp.swapaxes(w_in, 1, 2),
                     ((0, 0), (0, IN_ROWS - w_in.shape[2]), (0, 0))).astype(BF16)
    w_out_b = w_out.astype(BF16)
    w_gate_b = w_gate.astype(BF16)
    w_up_b = w_up.astype(BF16)
    w_down_b = w_down.astype(BF16)
    bias = jnp.pad(forget_bias.astype(F32), ((0, 0), (0, BF16_ROWS - N_HEADS_FOX)))[:, :, None]
    lam_vecs = jnp.pad(jnp.stack([lam_q1, lam_k1, lam_q2, lam_k2], axis=1).astype(F32),
                       ((0, 0), (0, SUBLANES - 4), (0, LANES - HEAD_DIM)))
    row = lambda v: v.reshape(1, -1).astype(F32)

    for l in range(depth):
        lam_init = 0.8 - 0.6 * math.exp(-0.3 * l)
        qd, kd, vd, qf, kf, vf = _project(x, row(attn_pre_g[l]), w_in_t[l], cos, sin, bias[l], t)
        oa = _diff_attention(lam_vecs[l], diff_sub_g[l].reshape(-1, 1).astype(F32), qd, kd, vd,
                             t, lam_init)
        ob = _fox_attention(qf, kf, vf, t)
        x = _mix_ffn(oa, ob, x, w_out_b[l, :DIFF_WIDTH], w_out_b[l, DIFF_WIDTH:],
                     row(attn_post_g[l]), row(ffn_pre_g[l]), w_gate_b[l], w_up_b[l], w_down_b[l],
                     row(ffn_post_g[l]), t)
    return x
```

```python
import functools
import math

import jax
import jax.numpy as jnp
from jax import lax
from jax.experimental import pallas as pl
from jax.experimental.pallas import tpu as pltpu

D_MODEL = 1024
HEAD_DIM = 64
N_HEADS_DIFF = 4
N_HEADS_FOX = 8
DIFF_WIDTH = N_HEADS_DIFF * 2 * HEAD_DIM
FOX_WIDTH = N_HEADS_FOX * HEAD_DIM
ROT_DIM = HEAD_DIM // 4
ROT_HALF = ROT_DIM // 2
ROPE_THETA = 500000.0
D_FF = 2816
EPS = 1e-6
NEG_INF = -1e30
LOG2E = 1.4426950408889634
Q_SCALE = HEAD_DIM ** -0.5 * LOG2E

LANES = 128
SUBLANES = 8
BF16_ROWS = 16
F_ROW = 3 * DIFF_WIDTH + 3 * FOX_WIDTH
IN_ROWS = F_ROW + BF16_ROWS
V_DIFF_ROWS = 2 * HEAD_DIM + BF16_ROWS
V_FOX_ROWS = HEAD_DIM + BF16_ROWS
VMEM_LIMIT = 56 * 1024 * 1024

F32 = jnp.float32
BF16 = jnp.bfloat16


def _tile(seq):
    return min(512, seq)


def _dot(a, b):
    return jnp.dot(a, b, preferred_element_type=F32)


def _rms(x, g):
    ms = jnp.mean(x * x, axis=-1, keepdims=True)
    return x * lax.rsqrt(ms + EPS) * g


def _split3(c):
    hi = c.astype(BF16).astype(F32)
    r = c - hi
    mid = r.astype(BF16).astype(F32)
    lo = r - mid
    return hi, mid, lo


ROPE_TILE = 2048


def _rope_kernel(pos_ref, invf_ref, cos_ref, sin_ref):
    ang = invf_ref[...] * pos_ref[...].astype(F32)
    cos_ref[...] = jnp.cos(ang)
    sin_ref[...] = jnp.sin(ang)


def _rope_tables(positions, tm):
    b, s = positions.shape
    inv_freq = 1.0 / (ROPE_THETA ** (jnp.arange(0, ROT_DIM, 2, dtype=F32) / ROT_DIM))
    spec = pl.BlockSpec((None, ROT_HALF, tm), lambda i, j: (i, 0, j))
    return pl.pallas_call(
        _rope_kernel,
        out_shape=(jax.ShapeDtypeStruct((b, ROT_HALF, s), F32),) * 2,
        grid=(b, s // tm),
        in_specs=[pl.BlockSpec((None, 1, tm), lambda i, j: (i, 0, j)),
                  pl.BlockSpec((ROT_HALF, 1), lambda i, j: (0, 0))],
        out_specs=(spec, spec),
        name="rope_tables",
    )(positions.reshape(b, 1, s), inv_freq.reshape(ROT_HALF, 1))


COL_GROUPS = 2


def _proj_kernel(x_ref, g_ref, wt_ref, cos_ref, sin_ref, bias_ref,
                 qd_ref, kd_ref, vd_ref, qf_ref, kf_ref, vf_ref, carry_ref):
    tm = x_ref.shape[0]
    tg = tm // COL_GROUPS
    groups = [slice(c0, c0 + tg) for c0 in range(0, tm, tg)]

    @pl.when(pl.program_id(1) == 0)
    def _():
        carry_ref[...] = jnp.zeros_like(carry_ref)

    h_ts = [_rms(x_ref[cs, :], g_ref[...]).T.astype(BF16) for cs in groups]
    ones_blk = jnp.ones((BF16_ROWS, tg), F32)

    def seg(gi, r0, n):
        return _dot(wt_ref[r0:r0 + n, :], h_ts[gi])

    def rot(y, cs):
        cosv = cos_ref[:, cs]
        sinv = sin_ref[:, cs]
        t1 = y[0:ROT_HALF]
        t2 = y[ROT_HALF:ROT_DIM]
        return jnp.concatenate([t1 * cosv - t2 * sinv, t2 * cosv + t1 * sinv, y[ROT_DIM:]], axis=0)

    def rot_head(y, cs):
        return jnp.concatenate([rot(y[0:HEAD_DIM], cs), rot(y[HEAD_DIM:], cs)], axis=0)

    for gi, cs in enumerate(groups):
        y = seg(gi, 0, DIFF_WIDTH)
        for hd in range(N_HEADS_DIFF):
            qd_ref[hd, :, cs] = (rot_head(y[hd * LANES:(hd + 1) * LANES], cs) * Q_SCALE).astype(BF16)
    for gi, cs in enumerate(groups):
        y = seg(gi, DIFF_WIDTH, DIFF_WIDTH)
        for hd in range(N_HEADS_DIFF):
            kd_ref[hd, cs, :] = rot_head(y[hd * LANES:(hd + 1) * LANES], cs).T.astype(BF16)
    for gi, cs in enumerate(groups):
        y = seg(gi, 2 * DIFF_WIDTH, DIFF_WIDTH)
        for hd in range(N_HEADS_DIFF):
            vd_ref[hd, gi] = jnp.concatenate(
                [y[hd * LANES:(hd + 1) * LANES], ones_blk], axis=0).astype(BF16)

    row16 = lax.broadcasted_iota(jnp.int32, (BF16_ROWS, tg), 0)
    tri = (lax.broadcasted_iota(jnp.int32, (tg, tg), 0)
           <= lax.broadcasted_iota(jnp.int32, (tg, tg), 1)).astype(BF16)
    carry = carry_ref[...]
    cs_all = []
    for gi in range(COL_GROUPS):
        z = seg(gi, F_ROW, BF16_ROWS) + bias_ref[...]
        log_f = -(jnp.maximum(-z, 0.0) + jnp.log1p(jnp.exp(-jnp.abs(z))))
        log_f = jnp.where(row16 < N_HEADS_FOX, log_f, 0.0)
        packed = jnp.concatenate(_split3(log_f), axis=0).astype(BF16)
        c3 = _dot(packed, tri)
        c = (c3[0:BF16_ROWS] + c3[BF16_ROWS:2 * BF16_ROWS] + c3[2 * BF16_ROWS:]
             + jnp.tile(carry, (1, tg // LANES)))
        carry = jnp.broadcast_to(c[:, tg - 1:tg], carry_ref.shape)
        cs_all.append(c)
    carry_ref[...] = carry

    row8 = lax.broadcasted_iota(jnp.int32, (SUBLANES, tg), 0)
    zpad = jnp.zeros((LANES - HEAD_DIM - SUBLANES, tg), F32)
    for gi, cs in enumerate(groups):
        yq = seg(gi, 3 * DIFF_WIDTH, FOX_WIDTH)
        yk = seg(gi, 3 * DIFF_WIDTH + FOX_WIDTH, FOX_WIDTH)
        yv = seg(gi, 3 * DIFF_WIDTH + 2 * FOX_WIDTH, FOX_WIDTH)
        c = cs_all[gi]
        for hf in range(N_HEADS_FOX):
            c_hi, c_mid, c_lo = _split3(
                jnp.broadcast_to(c[hf:hf + 1, :] * LOG2E, (SUBLANES, tg)))
            aug_q = jnp.where(row8 == 0, c_hi, jnp.where(row8 == 1, c_mid, jnp.where(
                row8 == 2, c_lo, jnp.where(row8 < 6, 1.0, 0.0))))
            aug_k = jnp.where(row8 < 3, 1.0, jnp.where(row8 == 3, -c_hi, jnp.where(
                row8 == 4, -c_mid, jnp.where(row8 == 5, -c_lo, 0.0))))
            feat = slice(hf * HEAD_DIM, (hf + 1) * HEAD_DIM)
            qf_ref[hf, :, cs] = jnp.concatenate(
                [yq[feat] * Q_SCALE, aug_q, zpad], axis=0).astype(BF16)
            kf_ref[hf, cs, :] = jnp.concatenate([yk[feat], aug_k, zpad], axis=0).T.astype(BF16)
            vf_ref[hf, gi] = jnp.concatenate([yv[feat], ones_blk], axis=0).astype(BF16)


def _project(x, g, wt, cos, sin, bias, t):
    b, s, _ = x.shape
    tm = COL_GROUPS * t
    tok = lambda i, j: (i, j, 0)
    const = lambda i, j: (0, 0)
    rope = pl.BlockSpec((None, ROT_HALF, tm), lambda i, j: (i, 0, j))
    sd = jax.ShapeDtypeStruct

    def specs(n, v_rows):
        return (
            (sd((b, n, LANES, s), BF16), pl.BlockSpec((None, n, LANES, tm), lambda i, j: (i, 0, 0, j))),
            (sd((b, n, s, LANES), BF16), pl.BlockSpec((None, n, tm, LANES), lambda i, j: (i, 0, j, 0))),
            (sd((b, n, s // t, v_rows, t), BF16),
             pl.BlockSpec((None, n, COL_GROUPS, v_rows, t), lambda i, j: (i, 0, j, 0, 0))),
        )

    outs = specs(N_HEADS_DIFF, V_DIFF_ROWS) + specs(N_HEADS_FOX, V_FOX_ROWS)
    return pl.pallas_call(
        _proj_kernel,
        out_shape=tuple(o[0] for o in outs),
        grid=(b, s // tm),
        in_specs=[pl.BlockSpec((None, tm, D_MODEL), tok),
                  pl.BlockSpec((1, D_MODEL), const),
                  pl.BlockSpec((IN_ROWS, D_MODEL), const, pipeline_mode=pl.Buffered(1)),
                  rope, rope,
                  pl.BlockSpec((BF16_ROWS, 1), const)],
        out_specs=tuple(o[1] for o in outs),
        scratch_shapes=[pltpu.VMEM((BF16_ROWS, LANES), F32)],
        compiler_params=pltpu.CompilerParams(
            dimension_semantics=("arbitrary", "arbitrary"), vmem_limit_bytes=VMEM_LIMIT),
        name="in_proj",
    )(x, g, wt, cos, sin, bias)


DIAG_BUF = 2
SCORE_PAD = LANES


def _attend(qi, n_blocks, q_cur, q_next, k_at, v_at, s_buf, mc_buf, m_sc, acc_sc):
    n = len(q_cur)
    t = m_sc.shape[-1]
    half = t // 2
    lo, hi = slice(0, half), slice(half, t)
    m_sc[...] = jnp.full_like(m_sc, NEG_INF)
    acc_sc[...] = jnp.zeros_like(acc_sc)

    def stage_a(i, buf, q_ts, blk):
        s_t = _dot(k_at(i, blk), q_ts[i]())
        s_buf[buf, i, :, 0:t] = s_t
        mc_buf[buf, i] = jnp.max(s_t, axis=0, keepdims=True)

    def stage_bc(i, buf, blk):
        m_prev = m_sc[i]
        m_new = jnp.maximum(m_prev, mc_buf[buf, i])
        alpha = jnp.exp2(m_prev - m_new)
        p_t = jnp.exp2(s_buf[buf, i, :, 0:t] - m_new).astype(BF16)
        acc_sc[i] = alpha * acc_sc[i] + _dot(v_at(i, blk), p_t)
        m_sc[i] = m_new

    def stage_a_diag(i, q_ts, blk):
        q_t = q_ts[i]()
        k = k_at(i, blk)
        keep = (lax.broadcasted_iota(jnp.int32, (half, half), 1)
                >= lax.broadcasted_iota(jnp.int32, (half, half), 0))
        s_ll = jnp.where(keep, _dot(k[lo], q_t[:, lo]), NEG_INF)
        s_lh = _dot(k[lo], q_t[:, hi])
        s_hh = jnp.where(keep, _dot(k[hi], q_t[:, hi]), NEG_INF)
        s_buf[DIAG_BUF, i, lo, lo] = s_ll
        s_buf[DIAG_BUF, i, lo, hi] = s_lh
        s_buf[DIAG_BUF, i, hi, hi] = s_hh
        mc_buf[DIAG_BUF, i, :, lo] = jnp.max(s_ll, axis=0, keepdims=True)
        mc_buf[DIAG_BUF, i, :, hi] = jnp.maximum(jnp.max(s_lh, axis=0, keepdims=True),
                                                 jnp.max(s_hh, axis=0, keepdims=True))

    def stage_bc_diag(i, blk):
        m_prev = m_sc[i]
        m_new = jnp.maximum(m_prev, mc_buf[DIAG_BUF, i])
        alpha = jnp.exp2(m_prev - m_new)
        v_t = v_at(i, blk)
        p_lo = jnp.exp2(s_buf[DIAG_BUF, i, lo, lo] - m_new[:, lo]).astype(BF16)
        p_hi = jnp.exp2(s_buf[DIAG_BUF, i, :, hi] - m_new[:, hi]).astype(BF16)
        acc_sc[i, :, lo] = alpha[:, lo] * acc_sc[i, :, lo] + _dot(v_t[:, lo], p_lo)
        acc_sc[i, :, hi] = alpha[:, hi] * acc_sc[i, :, hi] + _dot(v_t, p_hi)
        m_sc[i] = m_new

    def overlap(a=None, bc=None):
        for i in range(n):
            if a is not None:
                buf, q_ts, blk = a
                stage_a_diag(i, q_ts, blk) if buf == DIAG_BUF else stage_a(i, buf, q_ts, blk)
            if bc is not None:
                buf, blk = bc
                stage_bc_diag(i, blk) if buf == DIAG_BUF else stage_bc(i, buf, blk)

    next_diag = (DIAG_BUF, q_next, jnp.minimum(qi + 1, n_blocks - 1))

    @pl.when(qi == 0)
    def _():
        overlap(a=(DIAG_BUF, q_cur, qi))
        overlap(bc=(DIAG_BUF, qi))
        overlap(a=next_diag)

    @pl.when(qi == 1)
    def _():
        overlap(a=(1, q_cur, 0), bc=(DIAG_BUF, qi))
        overlap(a=next_diag, bc=(1, 0))

    @pl.when(qi >= 2)
    def _():
        overlap(a=(1, q_cur, 0), bc=(DIAG_BUF, qi))
        overlap(a=(0, q_cur, 1), bc=(1, 0))

        def pair(ii):
            i = 2 * ii
            overlap(a=(1, q_cur, i), bc=(0, i - 1))
            overlap(a=(0, q_cur, i + 1), bc=(1, i))

        n_pairs = qi // 2 - 1

        def two_pairs(jj, carry):
            pair(1 + 2 * jj)
            pair(2 + 2 * jj)
            return carry

        lax.fori_loop(0, n_pairs // 2, two_pairs, 0)

        @pl.when(n_pairs % 2 == 1)
        def _():
            pair(n_pairs)

        @pl.when(qi % 2 == 1)
        def _():
            overlap(a=(1, q_cur, qi - 1), bc=(0, qi - 2))
            overlap(a=next_diag, bc=(1, qi - 1))

        @pl.when(qi % 2 == 0)
        def _():
            overlap(a=next_diag, bc=(0, qi - 1))


MAPS_PER_STEP = 4
DIFF_GROUP = MAPS_PER_STEP // 2
FOX_GROUP = MAPS_PER_STEP


def _next_q_block(n_blocks):
    return lambda i, h, j: (i, h, 0, jnp.minimum(j + 1, n_blocks - 1))


def _attend_scratch(v_rows, t):
    n = MAPS_PER_STEP
    return [pltpu.VMEM((DIAG_BUF + 1, n, t, t + SCORE_PAD), F32),
            pltpu.VMEM((DIAG_BUF + 1, n, 1, t), F32),
            pltpu.VMEM((n, 1, t), F32), pltpu.VMEM((n, v_rows, t), F32)]


def _diff_attn_kernel(lam_ref, g_ref, q_ref, qn_ref, k_ref, v_ref, o_ref,
                      s_buf, mc_buf, m_sc, acc_sc, *, lam_init):
    t = q_ref.shape[2]
    qi = pl.program_id(2)
    zero = jnp.zeros((HEAD_DIM, t), BF16)

    def maps_of(ref):
        q_ts = []
        for hd in range(DIFF_GROUP):
            q_ts.append(lambda hd=hd: jnp.concatenate([ref[hd, 0:HEAD_DIM, :], zero], axis=0))
            q_ts.append(lambda hd=hd: jnp.concatenate([zero, ref[hd, HEAD_DIM:, :]], axis=0))
        return q_ts

    k_at = lambda i, j: k_ref[i // 2, pl.ds(pl.multiple_of(j * t, t), t), :]
    v_at = lambda i, j: v_ref[i // 2, j]
    _attend(qi, v_ref.shape[1], maps_of(q_ref), maps_of(qn_ref), k_at, v_at,
            s_buf, mc_buf, m_sc, acc_sc)

    lam = (jnp.exp(jnp.sum(lam_ref[0:1, :] * lam_ref[1:2, :], axis=1, keepdims=True))
           - jnp.exp(jnp.sum(lam_ref[2:3, :] * lam_ref[3:4, :], axis=1, keepdims=True)) + lam_init)
    nv = 2 * HEAD_DIM
    outs = []
    for hd in range(DIFF_GROUP):
        a0 = acc_sc[2 * hd]
        a1 = acc_sc[2 * hd + 1]
        o_t = a0[0:nv] / a0[nv:nv + 1] - lam * (a1[0:nv] / a1[nv:nv + 1])
        ms = jnp.mean(o_t * o_t, axis=0, keepdims=True)
        outs.append(o_t * lax.rsqrt(ms + EPS) * g_ref[...] * (1.0 - lam_init))
    o_ref[...] = jnp.concatenate(outs, axis=0).T.astype(o_ref.dtype)


def _diff_attention(lam_vecs, g_col, qd, kd, vd, t, lam_init):
    b, nh, s, _ = kd.shape
    nkb = s // t
    g = DIFF_GROUP
    return pl.pallas_call(
        functools.partial(_diff_attn_kernel, lam_init=lam_init),
        out_shape=jax.ShapeDtypeStruct((b, s, DIFF_WIDTH), BF16),
        grid=(b, nh // g, nkb),
        in_specs=[pl.BlockSpec((SUBLANES, LANES), lambda i, h, j: (0, 0)),
                  pl.BlockSpec((LANES, 1), lambda i, h, j: (0, 0)),
                  pl.BlockSpec((None, g, LANES, t), lambda i, h, j: (i, h, 0, j)),
                  pl.BlockSpec((None, g, LANES, t), _next_q_block(nkb)),
                  pl.BlockSpec((None, g, s, LANES), lambda i, h, j: (i, h, 0, 0)),
                  pl.BlockSpec((None, g, nkb, V_DIFF_ROWS, t), lambda i, h, j: (i, h, 0, 0, 0))],
        out_specs=pl.BlockSpec((None, t, g * 2 * HEAD_DIM), lambda i, h, j: (i, j, h)),
        scratch_shapes=_attend_scratch(V_DIFF_ROWS, t),
        compiler_params=pltpu.CompilerParams(
            dimension_semantics=("arbitrary",) * 3, vmem_limit_bytes=VMEM_LIMIT),
        name="diff_attention",
    )(lam_vecs, g_col, qd, qd, kd, vd)


def _fox_attn_kernel(q_ref, qn_ref, k_ref, v_ref, o_ref, s_buf, mc_buf, m_sc, acc_sc):
    t = q_ref.shape[2]
    qi = pl.program_id(2)
    maps_of = lambda ref: tuple((lambda hh=hh: ref[hh]) for hh in range(FOX_GROUP))
    k_at = lambda i, j: k_ref[i, pl.ds(pl.multiple_of(j * t, t), t), :]
    v_at = lambda i, j: v_ref[i, j]
    _attend(qi, v_ref.shape[1], maps_of(q_ref), maps_of(qn_ref), k_at, v_at,
            s_buf, mc_buf, m_sc, acc_sc)

    outs = []
    for hh in range(FOX_GROUP):
        a = acc_sc[hh]
        outs.append(a[0:HEAD_DIM] / a[HEAD_DIM:HEAD_DIM + 1])
    o_ref[...] = jnp.concatenate(outs, axis=0).T.astype(o_ref.dtype)


def _fox_attention(qf, kf, vf, t):
    b, nh, s, _ = kf.shape
    nkb = s // t
    g = FOX_GROUP
    return pl.pallas_call(
        _fox_attn_kernel,
        out_shape=jax.ShapeDtypeStruct((b, s, FOX_WIDTH), BF16),
        grid=(b, nh // g, nkb),
        in_specs=[pl.BlockSpec((None, g, LANES, t), lambda i, h, j: (i, h, 0, j)),
                  pl.BlockSpec((None, g, LANES, t), _next_q_block(nkb)),
                  pl.BlockSpec((None, g, s, LANES), lambda i, h, j: (i, h, 0, 0)),
                  pl.BlockSpec((None, g, nkb, V_FOX_ROWS, t), lambda i, h, j: (i, h, 0, 0, 0))],
        out_specs=pl.BlockSpec((None, t, g * HEAD_DIM), lambda i, h, j: (i, j, h)),
        scratch_shapes=_attend_scratch(V_FOX_ROWS, t),
        compiler_params=pltpu.CompilerParams(
            dimension_semantics=("arbitrary",) * 3, vmem_limit_bytes=VMEM_LIMIT),
        name="fox_attention",
    )(qf, qf, kf, vf)


MXU_TILE = 256
FF_CHUNKS = ((0, 6 * MXU_TILE), (6 * MXU_TILE, D_FF))
ROW_SPLIT = 2


def _mix_ffn_kernel(oa_ref, ob_ref, x_ref, wa_ref, wb_ref, gmix_ref, gpre_ref,
                    wg_ref, wu_ref, wd_ref, gpost_ref, o_ref):
    tm = x_ref.shape[0]
    halves = [slice(r0, r0 + tm // ROW_SPLIT) for r0 in range(0, tm, tm // ROW_SPLIT)]
    xs, hs = [], []
    for rs in halves:
        y = _dot(oa_ref[rs, :], wa_ref[...]) + _dot(ob_ref[rs, :], wb_ref[...])
        x = x_ref[rs, :] + _rms(y, gmix_ref[...])
        xs.append(x)
        hs.append(_rms(x, gpre_ref[...]).astype(BF16))
    ys = [None] * len(halves)
    for c0, c1 in FF_CHUNKS:
        for k in range(len(halves)):
            gate = _dot(hs[k], wg_ref[:, c0:c1])
            up = _dot(hs[k], wu_ref[:, c0:c1])
            act = (gate * jax.nn.sigmoid(gate) * up).astype(BF16)
            part = _dot(act, wd_ref[c0:c1, :])
            ys[k] = part if ys[k] is None else ys[k] + part
    for k, rs in enumerate(halves):
        o_ref[rs, :] = xs[k] + _rms(ys[k], gpost_ref[...])


def _mix_ffn(oa, ob, x, wa, wb, gmix, gpre, wg, wu, wd, gpost, tm):
    b, s, _ = x.shape
    tok = lambda i, j: (i, j, 0)
    const = lambda i, j: (0, 0)
    resident = lambda shape: pl.BlockSpec(shape, const, pipeline_mode=pl.Buffered(1))
    gain = pl.BlockSpec((1, D_MODEL), const)
    return pl.pallas_call(
        _mix_ffn_kernel,
        out_shape=jax.ShapeDtypeStruct(x.shape, F32),
        grid=(b, s // tm),
        in_specs=[pl.BlockSpec((None, tm, DIFF_WIDTH), tok),
                  pl.BlockSpec((None, tm, FOX_WIDTH), tok),
                  pl.BlockSpec((None, tm, D_MODEL), tok),
                  resident((DIFF_WIDTH, D_MODEL)), resident((FOX_WIDTH, D_MODEL)), gain, gain,
                  resident((D_MODEL, D_FF)), resident((D_MODEL, D_FF)), resident((D_FF, D_MODEL)),
                  gain],
        out_specs=pl.BlockSpec((None, tm, D_MODEL), tok),
        compiler_params=pltpu.CompilerParams(
            dimension_semantics=("arbitrary", "arbitrary"), vmem_limit_bytes=VMEM_LIMIT),
        name="mix_ffn",
    )(oa, ob, x, wa, wb, gmix, gpre, wg, wu, wd, gpost)


def kernel(x, positions, attn_pre_g, w_in, forget_bias, lam_q1, lam_k1, lam_q2, lam_k2,
           diff_sub_g, w_out, attn_post_g, ffn_pre_g, w_gate, w_up, w_down, ffn_post_g):
    depth = w_in.shape[0]
    seq = x.shape[1]
    t = _tile(seq)
    assert seq % (COL_GROUPS * t) == 0 and t % LANES == 0 and x.shape[2] == D_MODEL

    cos, sin = _rope_tables(positions, ROPE_TILE if seq % ROPE_TILE == 0 else t)
    w_in_t = jnp.pad(jnp.swapaxes(w_in, 1, 2),
                     ((0, 0), (0, IN_ROWS - w_in.shape[2]), (0, 0))).astype(BF16)
    w_out_b = w_out.astype(BF16)
    w_gate_b = w_gate.astype(BF16)
    w_up_b = w_up.astype(BF16)
    w_down_b = w_down.astype(BF16)
    bias = jnp.pad(forget_bias.astype(F32), ((0, 0), (0, BF16_ROWS - N_HEADS_FOX)))[:, :, None]
    lam_vecs = jnp.pad(jnp.stack([lam_q1, lam_k1, lam_q2, lam_k2], axis=1).astype(F32),
                       ((0, 0), (0, SUBLANES - 4), (0, LANES - HEAD_DIM)))
    row = lambda v: v.reshape(1, -1).astype(F32)

    for l in range(depth):
        lam_init = 0.8 - 0.6 * math.exp(-0.3 * l)
        qd, kd, vd, qf, kf, vf = _project(x, row(attn_pre_g[l]), w_in_t[l], cos, sin, bias[l], t)
        oa = _diff_attention(lam_vecs[l], diff_sub_g[l].reshape(-1, 1).astype(F32), qd, kd, vd,
                             t, lam_init)
        ob = _fox_attention(qf, kf, vf, t)
        x = _mix_ffn(oa, ob, x, w_out_b[l, :DIFF_WIDTH], w_out_b[l, DIFF_WIDTH:],
                     row(attn_post_g[l]), row(ffn_pre_g[l]), w_gate_b[l], w_up_b[l], w_down_b[l],
                     row(ffn_post_g[l]), t)
    return x
```

```python
import functools
import math

import jax
import jax.numpy as jnp
from jax import lax
from jax.experimental import pallas as pl
from jax.experimental.pallas import tpu as pltpu

D_MODEL = 1024
HEAD_DIM = 64
N_HEADS_DIFF = 4
N_HEADS_FOX = 8
DIFF_WIDTH = N_HEADS_DIFF * 2 * HEAD_DIM
FOX_WIDTH = N_HEADS_FOX * HEAD_DIM
ROT_DIM = HEAD_DIM // 4
ROT_HALF = ROT_DIM // 2
ROPE_THETA = 500000.0
D_FF = 2816
EPS = 1e-6
NEG_INF = -1e30
LOG2E = 1.4426950408889634
Q_SCALE = HEAD_DIM ** -0.5 * LOG2E

LANES = 128
SUBLANES = 8
BF16_ROWS = 16
F_ROW = 3 * DIFF_WIDTH + 3 * FOX_WIDTH
IN_ROWS = F_ROW + BF16_ROWS
V_DIFF_ROWS = 2 * HEAD_DIM + BF16_ROWS
V_FOX_ROWS = HEAD_DIM + BF16_ROWS
VMEM_LIMIT = 56 * 1024 * 1024

F32 = jnp.float32
BF16 = jnp.bfloat16


def _tile(seq):
    return min(512, seq)


def _dot(a, b):
    return jnp.dot(a, b, preferred_element_type=F32)


def _rms(x, g):
    ms = jnp.mean(x * x, axis=-1, keepdims=True)
    return x * lax.rsqrt(ms + EPS) * g


def _split3(c):
    hi = c.astype(BF16).astype(F32)
    r = c - hi
    mid = r.astype(BF16).astype(F32)
    lo = r - mid
    return hi, mid, lo


ROPE_TILE = 2048


def _rope_kernel(pos_ref, invf_ref, cos_ref, sin_ref):
    ang = invf_ref[...] * pos_ref[...].astype(F32)
    cos_ref[...] = jnp.cos(ang)
    sin_ref[...] = jnp.sin(ang)


def _rope_tables(positions, tm):
    b, s = positions.shape
    inv_freq = 1.0 / (ROPE_THETA ** (jnp.arange(0, ROT_DIM, 2, dtype=F32) / ROT_DIM))
    spec = pl.BlockSpec((None, ROT_HALF, tm), lambda i, j: (i, 0, j))
    return pl.pallas_call(
        _rope_kernel,
        out_shape=(jax.ShapeDtypeStruct((b, ROT_HALF, s), F32),) * 2,
        grid=(b, s // tm),
        in_specs=[pl.BlockSpec((None, 1, tm), lambda i, j: (i, 0, j)),
                  pl.BlockSpec((ROT_HALF, 1), lambda i, j: (0, 0))],
        out_specs=(spec, spec),
        name="rope_tables",
    )(positions.reshape(b, 1, s), inv_freq.reshape(ROT_HALF, 1))


COL_GROUPS = 2


def _proj_kernel(x_ref, g_ref, wt_ref, cos_ref, sin_ref, bias_ref,
                 qd_ref, kd_ref, vd_ref, qf_ref, kf_ref, vf_ref, carry_ref):
    tm = x_ref.shape[0]
    tg = tm // COL_GROUPS
    groups = [slice(c0, c0 + tg) for c0 in range(0, tm, tg)]

    @pl.when(pl.program_id(1) == 0)
    def _():
        carry_ref[...] = jnp.zeros_like(carry_ref)

    h_ts = [_rms(x_ref[cs, :], g_ref[...]).T.astype(BF16) for cs in groups]
    ones_blk = jnp.ones((BF16_ROWS, tg), F32)

    def seg(gi, r0, n):
        return _dot(wt_ref[r0:r0 + n, :], h_ts[gi])

    def rot(y, cs):
        cosv = cos_ref[:, cs]
        sinv = sin_ref[:, cs]
        t1 = y[0:ROT_HALF]
        t2 = y[ROT_HALF:ROT_DIM]
        return jnp.concatenate([t1 * cosv - t2 * sinv, t2 * cosv + t1 * sinv, y[ROT_DIM:]], axis=0)

    def rot_head(y, cs):
        return jnp.concatenate([rot(y[0:HEAD_DIM], cs), rot(y[HEAD_DIM:], cs)], axis=0)

    for gi, cs in enumerate(groups):
        y = seg(gi, 0, DIFF_WIDTH)
        for hd in range(N_HEADS_DIFF):
            qd_ref[hd, :, cs] = (rot_head(y[hd * LANES:(hd + 1) * LANES], cs) * Q_SCALE).astype(BF16)
    for gi, cs in enumerate(groups):
        y = seg(gi, DIFF_WIDTH, DIFF_WIDTH)
        for hd in range(N_HEADS_DIFF):
            kd_ref[hd, cs, :] = rot_head(y[hd * LANES:(hd + 1) * LANES], cs).T.astype(BF16)
    for gi, cs in enumerate(groups):
        y = seg(gi, 2 * DIFF_WIDTH, DIFF_WIDTH)
        for hd in range(N_HEADS_DIFF):
            vd_ref[hd, gi] = jnp.concatenate(
                [y[hd * LANES:(hd + 1) * LANES], ones_blk], axis=0).astype(BF16)

    row16 = lax.broadcasted_iota(jnp.int32, (BF16_ROWS, tg), 0)
    tri = (lax.broadcasted_iota(jnp.int32, (tg, tg), 0)
           <= lax.broadcasted_iota(jnp.int32, (tg, tg), 1)).astype(BF16)
    carry = carry_ref[...]
    cs_all = []
    for gi in range(COL_GROUPS):
        z = seg(gi, F_ROW, BF16_ROWS) + bias_ref[...]
        log_f = -(jnp.maximum(-z, 0.0) + jnp.log1p(jnp.exp(-jnp.abs(z))))
        log_f = jnp.where(row16 < N_HEADS_FOX, log_f, 0.0)
        packed = jnp.concatenate(_split3(log_f), axis=0).astype(BF16)
        c3 = _dot(packed, tri)
        c = (c3[0:BF16_ROWS] + c3[BF16_ROWS:2 * BF16_ROWS] + c3[2 * BF16_ROWS:]
             + jnp.tile(carry, (1, tg // LANES)))
        carry = jnp.broadcast_to(c[:, tg - 1:tg], carry_ref.shape)
        cs_all.append(c)
    carry_ref[...] = carry

    row8 = lax.broadcasted_iota(jnp.int32, (SUBLANES, tg), 0)
    zpad = jnp.zeros((LANES - HEAD_DIM - SUBLANES, tg), F32)
    for gi, cs in enumerate(groups):
        yq = seg(gi, 3 * DIFF_WIDTH, FOX_WIDTH)
        yk = seg(gi, 3 * DIFF_WIDTH + FOX_WIDTH, FOX_WIDTH)
        yv = seg(gi, 3 * DIFF_WIDTH + 2 * FOX_WIDTH, FOX_WIDTH)
        c = cs_all[gi]
        for hf in range(N_HEADS_FOX):
            c_hi, c_mid, c_lo = _split3(
                jnp.broadcast_to(c[hf:hf + 1, :] * LOG2E, (SUBLANES, tg)))
            aug_q = jnp.where(row8 == 0, c_hi, jnp.where(row8 == 1, c_mid, jnp.where(
                row8 == 2, c_lo, jnp.where(row8 < 6, 1.0, 0.0))))
            aug_k = jnp.where(row8 < 3, 1.0, jnp.where(row8 == 3, -c_hi, jnp.where(
                row8 == 4, -c_mid, jnp.where(row8 == 5, -c_lo, 0.0))))
            feat = slice(hf * HEAD_DIM, (hf + 1) * HEAD_DIM)
            qf_ref[hf, :, cs] = jnp.concatenate(
                [yq[feat] * Q_SCALE, aug_q, zpad], axis=0).astype(BF16)
            kf_ref[hf, cs, :] = jnp.concatenate([yk[feat], aug_k, zpad], axis=0).T.astype(BF16)
            vf_ref[hf, gi] = jnp.concatenate([yv[feat], ones_blk], axis=0).astype(BF16)


def _project(x, g, wt, cos, sin, bias, t):
    b, s, _ = x.shape
    tm = COL_GROUPS * t
    tok = lambda i, j: (i, j, 0)
    const = lambda i, j: (0, 0)
    rope = pl.BlockSpec((None, ROT_HALF, tm), lambda i, j: (i, 0, j))
    sd = jax.ShapeDtypeStruct

    def specs(n, v_rows):
        return (
            (sd((b, n, LANES, s), BF16), pl.BlockSpec((None, n, LANES, tm), lambda i, j: (i, 0, 0, j))),
            (sd((b, n, s, LANES), BF16), pl.BlockSpec((None, n, tm, LANES), lambda i, j: (i, 0, j, 0))),
            (sd((b, n, s // t, v_rows, t), BF16),
             pl.BlockSpec((None, n, COL_GROUPS, v_rows, t), lambda i, j: (i, 0, j, 0, 0))),
        )

    outs = specs(N_HEADS_DIFF, V_DIFF_ROWS) + specs(N_HEADS_FOX, V_FOX_ROWS)
    return pl.pallas_call(
        _proj_kernel,
        out_shape=tuple(o[0] for o in outs),
        grid=(b, s // tm),
        in_specs=[pl.BlockSpec((None, tm, D_MODEL), tok),
                  pl.BlockSpec((1, D_MODEL), const),
                  pl.BlockSpec((IN_ROWS, D_MODEL), const, pipeline_mode=pl.Buffered(1)),
                  rope, rope,
                  pl.BlockSpec((BF16_ROWS, 1), const)],
        out_specs=tuple(o[1] for o in outs),
        scratch_shapes=[pltpu.VMEM((BF16_ROWS, LANES), F32)],
        compiler_params=pltpu.CompilerParams(
            dimension_semantics=("arbitrary", "arbitrary"), vmem_limit_bytes=VMEM_LIMIT),
        name="in_proj",
    )(x, g, wt, cos, sin, bias)


DIAG_BUF = 2
SCORE_PAD = LANES


def _attend(qi, n_blocks, q_cur, q_next, k_at, v_at, s_buf, mc_buf, m_sc, acc_sc):
    n = len(q_cur)
    t = m_sc.shape[-1]
    half = t // 2
    lo, hi = slice(0, half), slice(half, t)
    m_sc[...] = jnp.full_like(m_sc, NEG_INF)
    acc_sc[...] = jnp.zeros_like(acc_sc)

    def stage_a(i, buf, q_ts, blk):
        s_t = _dot(k_at(i, blk), q_ts[i]())
        s_buf[buf, i, :, 0:t] = s_t
        mc_buf[buf, i] = jnp.max(s_t, axis=0, keepdims=True)

    def stage_bc(i, buf, blk):
        m_prev = m_sc[i]
        m_new = jnp.maximum(m_prev, mc_buf[buf, i])
        alpha = jnp.exp2(m_prev - m_new)
        p_t = jnp.exp2(s_buf[buf, i, :, 0:t] - m_new).astype(BF16)
        acc_sc[i] = alpha * acc_sc[i] + _dot(v_at(i, blk), p_t)
        m_sc[i] = m_new

    def stage_a_diag(i, q_ts, blk):
        q_t = q_ts[i]()
        k = k_at(i, blk)
        keep = (lax.broadcasted_iota(jnp.int32, (half, half), 1)
                >= lax.broadcasted_iota(jnp.int32, (half, half), 0))
        s_ll = jnp.where(keep, _dot(k[lo], q_t[:, lo]), NEG_INF)
        s_lh = _dot(k[lo], q_t[:, hi])
        s_hh = jnp.where(keep, _dot(k[hi], q_t[:, hi]), NEG_INF)
        s_buf[DIAG_BUF, i, lo, lo] = s_ll
        s_buf[DIAG_BUF, i, lo, hi] = s_lh
        s_buf[DIAG_BUF, i, hi, hi] = s_hh
        mc_buf[DIAG_BUF, i, :, lo] = jnp.max(s_ll, axis=0, keepdims=True)
        mc_buf[DIAG_BUF, i, :, hi] = jnp.maximum(jnp.max(s_lh, axis=0, keepdims=True),
                                                 jnp.max(s_hh, axis=0, keepdims=True))

    def stage_bc_diag(i, blk):
        m_prev = m_sc[i]
        m_new = jnp.maximum(m_prev, mc_buf[DIAG_BUF, i])
        alpha = jnp.exp2(m_prev - m_new)
        v_t = v_at(i, blk)
        p_lo = jnp.exp2(s_buf[DIAG_BUF, i, lo, lo] - m_new[:, lo]).astype(BF16)
        p_hi = jnp.exp2(s_buf[DIAG_BUF, i, :, hi] - m_new[:, hi]).astype(BF16)
        acc_sc[i, :, lo] = alpha[:, lo] * acc_sc[i, :, lo] + _dot(v_t[:, lo], p_lo)
        acc_sc[i, :, hi] = alpha[:, hi] * acc_sc[i, :, hi] + _dot(v_t, p_hi)
        m_sc[i] = m_new

    def overlap(a=None, bc=None):
        for i in range(n):
            if a is not None:
                buf, q_ts, blk = a
                stage_a_diag(i, q_ts, blk) if buf == DIAG_BUF else stage_a(i, buf, q_ts, blk)
            if bc is not None:
                buf, blk = bc
                stage_bc_diag(i, blk) if buf == DIAG_BUF else stage_bc(i, buf, blk)

    next_diag = (DIAG_BUF, q_next, jnp.minimum(qi + 1, n_blocks - 1))

    @pl.when(qi == 0)
    def _():
        overlap(a=(DIAG_BUF, q_cur, qi))
        overlap(bc=(DIAG_BUF, qi))
        overlap(a=next_diag)

    @pl.when(qi == 1)
    def _():
        overlap(a=(1, q_cur, 0), bc=(DIAG_BUF, qi))
        overlap(a=next_diag, bc=(1, 0))

    @pl.when(qi >= 2)
    def _():
        overlap(a=(1, q_cur, 0), bc=(DIAG_BUF, qi))
        overlap(a=(0, q_cur, 1), bc=(1, 0))

        def pair(ii):
            i = 2 * ii
            overlap(a=(1, q_cur, i), bc=(0, i - 1))
            overlap(a=(0, q_cur, i + 1), bc=(1, i))

        n_pairs = qi // 2 - 1

        def four_pairs(jj, carry):
            for u in range(4):
                pair(1 + 4 * jj + u)
            return carry

        lax.fori_loop(0, n_pairs // 4, four_pairs, 0)

        def rest(ii, carry):
            pair(ii)
            return carry

        lax.fori_loop(1 + 4 * (n_pairs // 4), n_pairs + 1, rest, 0)

        @pl.when(qi % 2 == 1)
        def _():
            overlap(a=(1, q_cur, qi - 1), bc=(0, qi - 2))
            overlap(a=next_diag, bc=(1, qi - 1))

        @pl.when(qi % 2 == 0)
        def _():
            overlap(a=next_diag, bc=(0, qi - 1))


MAPS_PER_STEP = 4
DIFF_GROUP = MAPS_PER_STEP // 2
FOX_GROUP = MAPS_PER_STEP


def _next_q_block(n_blocks):
    return lambda i, h, j: (i, h, 0, jnp.minimum(j + 1, n_blocks - 1))


def _attend_scratch(v_rows, t):
    n = MAPS_PER_STEP
    return [pltpu.VMEM((DIAG_BUF + 1, n, t, t + SCORE_PAD), F32),
            pltpu.VMEM((DIAG_BUF + 1, n, 1, t), F32),
            pltpu.VMEM((n, 1, t), F32), pltpu.VMEM((n, v_rows, t), F32)]


def _diff_attn_kernel(lam_ref, g_ref, q_ref, qn_ref, k_ref, v_ref, o_ref,
                      s_buf, mc_buf, m_sc, acc_sc, *, lam_init):
    t = q_ref.shape[2]
    qi = pl.program_id(2)
    zero = jnp.zeros((HEAD_DIM, t), BF16)

    def maps_of(ref):
        q_ts = []
        for hd in range(DIFF_GROUP):
            q_ts.append(lambda hd=hd: jnp.concatenate([ref[hd, 0:HEAD_DIM, :], zero], axis=0))
            q_ts.append(lambda hd=hd: jnp.concatenate([zero, ref[hd, HEAD_DIM:, :]], axis=0))
        return q_ts

    k_at = lambda i, j: k_ref[i // 2, pl.ds(pl.multiple_of(j * t, t), t), :]
    v_at = lambda i, j: v_ref[i // 2, j]
    _attend(qi, v_ref.shape[1], maps_of(q_ref), maps_of(qn_ref), k_at, v_at,
            s_buf, mc_buf, m_sc, acc_sc)

    lam = (jnp.exp(jnp.sum(lam_ref[0:1, :] * lam_ref[1:2, :], axis=1, keepdims=True))
           - jnp.exp(jnp.sum(lam_ref[2:3, :] * lam_ref[3:4, :], axis=1, keepdims=True)) + lam_init)
    nv = 2 * HEAD_DIM
    outs = []
    for hd in range(DIFF_GROUP):
        a0 = acc_sc[2 * hd]
        a1 = acc_sc[2 * hd + 1]
        o_t = a0[0:nv] / a0[nv:nv + 1] - lam * (a1[0:nv] / a1[nv:nv + 1])
        ms = jnp.mean(o_t * o_t, axis=0, keepdims=True)
        outs.append(o_t * lax.rsqrt(ms + EPS) * g_ref[...] * (1.0 - lam_init))
    o_ref[...] = jnp.concatenate(outs, axis=0).T.astype(o_ref.dtype)


def _diff_attention(lam_vecs, g_col, qd, kd, vd, t, lam_init):
    b, nh, s, _ = kd.shape
    nkb = s // t
    g = DIFF_GROUP
    return pl.pallas_call(
        functools.partial(_diff_attn_kernel, lam_init=lam_init),
        out_shape=jax.ShapeDtypeStruct((b, s, DIFF_WIDTH), BF16),
        grid=(b, nh // g, nkb),
        in_specs=[pl.BlockSpec((SUBLANES, LANES), lambda i, h, j: (0, 0)),
                  pl.BlockSpec((LANES, 1), lambda i, h, j: (0, 0)),
                  pl.BlockSpec((None, g, LANES, t), lambda i, h, j: (i, h, 0, j)),
                  pl.BlockSpec((None, g, LANES, t), _next_q_block(nkb)),
                  pl.BlockSpec((None, g, s, LANES), lambda i, h, j: (i, h, 0, 0)),
                  pl.BlockSpec((None, g, nkb, V_DIFF_ROWS, t), lambda i, h, j: (i, h, 0, 0, 0))],
        out_specs=pl.BlockSpec((None, t, g * 2 * HEAD_DIM), lambda i, h, j: (i, j, h)),
        scratch_shapes=_attend_scratch(V_DIFF_ROWS, t),
        compiler_params=pltpu.CompilerParams(
            dimension_semantics=("arbitrary",) * 3, vmem_limit_bytes=VMEM_LIMIT),
        name="diff_attention",
    )(lam_vecs, g_col, qd, qd, kd, vd)


def _fox_attn_kernel(q_ref, qn_ref, k_ref, v_ref, o_ref, s_buf, mc_buf, m_sc, acc_sc):
    t = q_ref.shape[2]
    qi = pl.program_id(2)
    maps_of = lambda ref: tuple((lambda hh=hh: ref[hh]) for hh in range(FOX_GROUP))
    k_at = lambda i, j: k_ref[i, pl.ds(pl.multiple_of(j * t, t), t), :]
    v_at = lambda i, j: v_ref[i, j]
    _attend(qi, v_ref.shape[1], maps_of(q_ref), maps_of(qn_ref), k_at, v_at,
            s_buf, mc_buf, m_sc, acc_sc)

    outs = []
    for hh in range(FOX_GROUP):
        a = acc_sc[hh]
        outs.append(a[0:HEAD_DIM] / a[HEAD_DIM:HEAD_DIM + 1])
    o_ref[...] = jnp.concatenate(outs, axis=0).T.astype(o_ref.dtype)


def _fox_attention(qf, kf, vf, t):
    b, nh, s, _ = kf.shape
    nkb = s // t
    g = FOX_GROUP
    return pl.pallas_call(
        _fox_attn_kernel,
        out_shape=jax.ShapeDtypeStruct((b, s, FOX_WIDTH), BF16),
        grid=(b, nh // g, nkb),
        in_specs=[pl.BlockSpec((None, g, LANES, t), lambda i, h, j: (i, h, 0, j)),
                  pl.BlockSpec((None, g, LANES, t), _next_q_block(nkb)),
                  pl.BlockSpec((None, g, s, LANES), lambda i, h, j: (i, h, 0, 0)),
                  pl.BlockSpec((None, g, nkb, V_FOX_ROWS, t), lambda i, h, j: (i, h, 0, 0, 0))],
        out_specs=pl.BlockSpec((None, t, g * HEAD_DIM), lambda i, h, j: (i, j, h)),
        scratch_shapes=_attend_scratch(V_FOX_ROWS, t),
        compiler_params=pltpu.CompilerParams(
            dimension_semantics=("arbitrary",) * 3, vmem_limit_bytes=VMEM_LIMIT),
        name="fox_attention",
    )(qf, qf, kf, vf)


MXU_TILE = 256
FF_CHUNKS = ((0, 6 * MXU_TILE), (6 * MXU_TILE, D_FF))
ROW_SPLIT = 2


def _mix_ffn_kernel(oa_ref, ob_ref, x_ref, wa_ref, wb_ref, gmix_ref, gpre_ref,
                    wg_ref, wu_ref, wd_ref, gpost_ref, o_ref):
    tm = x_ref.shape[0]
    halves = [slice(r0, r0 + tm // ROW_SPLIT) for r0 in range(0, tm, tm // ROW_SPLIT)]
    xs, hs = [], []
    for rs in halves:
        y = _dot(oa_ref[rs, :], wa_ref[...]) + _dot(ob_ref[rs, :], wb_ref[...])
        x = x_ref[rs, :] + _rms(y, gmix_ref[...])
        xs.append(x)
        hs.append(_rms(x, gpre_ref[...]).astype(BF16))
    ys = [None] * len(halves)
    for c0, c1 in FF_CHUNKS:
        for k in range(len(halves)):
            gate = _dot(hs[k], wg_ref[:, c0:c1])
            up = _dot(hs[k], wu_ref[:, c0:c1])
            act = (gate * jax.nn.sigmoid(gate) * up).astype(BF16)
            part = _dot(act, wd_ref[c0:c1, :])
            ys[k] = part if ys[k] is None else ys[k] + part
    for k, rs in enumerate(halves):
        o_ref[rs, :] = xs[k] + _rms(ys[k], gpost_ref[...])


def _mix_ffn(oa, ob, x, wa, wb, gmix, gpre, wg, wu, wd, gpost, tm):
    b, s, _ = x.shape
    tok = lambda i, j: (i, j, 0)
    const = lambda i, j: (0, 0)
    resident = lambda shape: pl.BlockSpec(shape, const, pipeline_mode=pl.Buffered(1))
    gain = pl.BlockSpec((1, D_MODEL), const)
    return pl.pallas_call(
        _mix_ffn_kernel,
        out_shape=jax.ShapeDtypeStruct(x.shape, F32),
        grid=(b, s // tm),
        in_specs=[pl.BlockSpec((None, tm, DIFF_WIDTH), tok),
                  pl.BlockSpec((None, tm, FOX_WIDTH), tok),
                  pl.BlockSpec((None, tm, D_MODEL), tok),
                  resident((DIFF_WIDTH, D_MODEL)), resident((FOX_WIDTH, D_MODEL)), gain, gain,
                  resident((D_MODEL, D_FF)), resident((D_MODEL, D_FF)), resident((D_FF, D_MODEL)),
                  gain],
        out_specs=pl.BlockSpec((None, tm, D_MODEL), tok),
        compiler_params=pltpu.CompilerParams(
            dimension_semantics=("arbitrary", "arbitrary"), vmem_limit_bytes=VMEM_LIMIT),
        name="mix_ffn",
    )(oa, ob, x, wa, wb, gmix, gpre, wg, wu, wd, gpost)


def kernel(x, positions, attn_pre_g, w_in, forget_bias, lam_q1, lam_k1, lam_q2, lam_k2,
           diff_sub_g, w_out, attn_post_g, ffn_pre_g, w_gate, w_up, w_down, ffn_post_g):
    depth = w_in.shape[0]
    seq = x.shape[1]
    t = _tile(seq)
    assert seq % (COL_GROUPS * t) == 0 and t % LANES == 0 and x.shape[2] == D_MODEL

    cos, sin = _rope_tables(positions, ROPE_TILE if seq % ROPE_TILE == 0 else t)
    w_in_t = jnp.pad(jnp.swapaxes(w_in, 1, 2),
                     ((0, 0), (0, IN_ROWS - w_in.shape[2]), (0, 0))).astype(BF16)
    w_out_b = w_out.astype(BF16)
    w_gate_b = w_gate.astype(BF16)
    w_up_b = w_up.astype(BF16)
    w_down_b = w_down.astype(BF16)
    bias = jnp.pad(forget_bias.astype(F32), ((0, 0), (0, BF16_ROWS - N_HEADS_FOX)))[:, :, None]
    lam_vecs = jnp.pad(jnp.stack([lam_q1, lam_k1, lam_q2, lam_k2], axis=1).astype(F32),
                       ((0, 0), (0, SUBLANES - 4), (0, LANES - HEAD_DIM)))
    row = lambda v: v.reshape(1, -1).astype(F32)

    for l in range(depth):
        lam_init = 0.8 - 0.6 * math.exp(-0.3 * l)
        qd, kd, vd, qf, kf, vf = _project(x, row(attn_pre_g[l]), w_in_t[l], cos, sin, bias[l], t)
        oa = _diff_attention(lam_vecs[l], diff_sub_g[l].reshape(-1, 1).astype(F32), qd, kd, vd,
                             t, lam_init)
        ob = _fox_attention(qf, kf, vf, t)
        x = _mix_ffn(oa, ob, x, w_out_b[l, :DIFF_WIDTH], w_out_b[l, DIFF_WIDTH:],
                     row(attn_post_g[l]), row(ffn_pre_g[l]), w_gate_b[l], w_up_b[l], w_down_b[l],
                     row(ffn_post_g[l]), t)
    return x
```
